```python
import math
import jax, jax.numpy as jnp
from jax import lax
import numpy as np

D_MODEL = 1024
BATCH = 4
SEQ = 4096
DEPTH = 4

GRID_W = 64
CTX_LEN = 256
N_MIXERS = 2
N_SSM_LAYERS = (DEPTH + 1) // 2
N_ATTN_LAYERS = DEPTH // 2
SSM_GROUP = 16
SSM_GROUPS = D_MODEL // SSM_GROUP
SSM_STATE = 64
SSM_DT_MIN = 1e-3
SSM_DT_MAX = 1e-1
HEAD_DIM = 64
N_Q_HEADS = D_MODEL // HEAD_DIM
N_KV_HEADS = 4
Q_PER_KV = N_Q_HEADS // N_KV_HEADS
ROPE_THETA = 10000.0
Q_BLOCK = 128
D_FF = 2816
N_EXPERTS = 8
TOP_K = 2
N_MOD = 6
EPS = 1e-6

kernel_name = 'hybrid_s5_gqa_moe_prefix_dit'


def _rmsnorm(x, g):
    xf = x.astype(jnp.float32)
    y = xf * lax.rsqrt(jnp.mean(xf * xf, axis=-1, keepdims=True) + EPS) * g.astype(jnp.float32)
    return y.astype(x.dtype)


def _modulation(cvec, w, b):
    return jnp.split(jax.nn.silu(cvec) @ w + b, N_MOD, axis=-1)


def _modulate(h, shift, scale):
    return h * (1.0 + scale) + shift


def _zoh(a_re, a_im, log_dt, b_re, b_im):
    a_re = a_re.astype(jnp.float32)
    a_im = a_im.astype(jnp.float32)
    dt = jnp.exp(log_dt.astype(jnp.float32))[:, None]
    mag = jnp.exp(a_re * dt)
    lam_re = mag * jnp.cos(a_im * dt)
    lam_im = mag * jnp.sin(a_im * dt)
    den = a_re * a_re + a_im * a_im
    num_re = lam_re - 1.0
    f_re = (num_re * a_re + lam_im * a_im) / den
    f_im = (lam_im * a_re - num_re * a_im) / den
    b_re = b_re.astype(jnp.float32)
    b_im = b_im.astype(jnp.float32)
    bbar_re = f_re[..., None] * b_re - f_im[..., None] * b_im
    bbar_im = f_re[..., None] * b_im + f_im[..., None] * b_re
    return lam_re, lam_im, bbar_re, bbar_im


def _ssm_combine(e1, e2):
    ar1, ai1, br1, bi1 = e1
    ar2, ai2, br2, bi2 = e2
    return (ar1 * ar2 - ai1 * ai2,
            ar1 * ai2 + ai1 * ar2,
            ar2 * br1 - ai2 * bi1 + br2,
            ar2 * bi1 + ai2 * br1 + bi2)


def _s5_scan(u, a_re, a_im, log_dt, b_re, b_im, s0, reverse):
    lam_re, lam_im, bbar_re, bbar_im = _zoh(a_re, a_im, log_dt, b_re, b_im)
    bu_re = jnp.einsum('bngi,gpi->bngp', u, bbar_re)
    bu_im = jnp.einsum('bngi,gpi->bngp', u, bbar_im)
    if s0 is not None:
        s0_re, s0_im = s0
        pos = -1 if reverse else 0
        bu_re = bu_re.at[:, pos].add(lam_re * s0_re - lam_im * s0_im)
        bu_im = bu_im.at[:, pos].add(lam_re * s0_im + lam_im * s0_re)
    n = u.shape[1]
    a_seq_re = jnp.broadcast_to(lam_re, (1, n) + lam_re.shape)
    a_seq_im = jnp.broadcast_to(lam_im, (1, n) + lam_im.shape)
    _, _, s_re, s_im = lax.associative_scan(
        _ssm_combine, (a_seq_re, a_seq_im, bu_re, bu_im), reverse=reverse, axis=1)
    return s_re, s_im


def _s5_readout(s_re, s_im, c_re, c_im):
    return (jnp.einsum('bngp,gip->bngi', s_re, c_re.astype(jnp.float32))
            - jnp.einsum('bngp,gip->bngi', s_im, c_im.astype(jnp.float32)))


def _s5_output(y, h, d, glu_w, glu_b):
    y = y + d.astype(jnp.float32) * h.astype(jnp.float32)
    z = jax.nn.gelu(y).astype(h.dtype)
    a, g = jnp.split(z @ glu_w + glu_b, 2, axis=-1)
    return a * jax.nn.sigmoid(g)


def _s5_mixer(hl, hc, a_re, a_im, log_dt, b_re, b_im, c_re, c_im, d, glu_w, glu_b, need_ctx):
    bsz, n_l, _ = hl.shape
    n_c = hc.shape[1]
    u_c = hc.astype(jnp.float32).reshape(bsz, n_c, SSM_GROUPS, SSM_GROUP)
    u_l = hl.astype(jnp.float32).reshape(bsz, n_l, SSM_GROUPS, SSM_GROUP)
    y_l = jnp.zeros((bsz, n_l, D_MODEL), jnp.float32)
    y_c = jnp.zeros((bsz, n_c, D_MODEL), jnp.float32)
    for r in range(2):
        reverse = r == 1
        sc_re, sc_im = _s5_scan(u_c, a_re[r], a_im[r], log_dt[r], b_re[r], b_im[r], None, reverse)
        fin = (sc_re[:, 0], sc_im[:, 0]) if reverse else (sc_re[:, -1], sc_im[:, -1])
        sl_re, sl_im = _s5_scan(u_l, a_re[r], a_im[r], log_dt[r], b_re[r], b_im[r], fin, reverse)
        y_l = y_l + _s5_readout(sl_re, sl_im, c_re[r], c_im[r]).reshape(bsz, n_l, D_MODEL)
        if need_ctx:
            y_c = y_c + _s5_readout(sc_re, sc_im, c_re[r], c_im[r]).reshape(bsz, n_c, D_MODEL)
    out_l = _s5_output(y_l, hl, d, glu_w, glu_b)
    out_c = _s5_output(y_c, hc, d, glu_w, glu_b) if need_ctx else None
    return out_l, out_c


def _axial_rope_tables(rows):
    row = jnp.broadcast_to(jnp.arange(rows)[:, None], (rows, GRID_W)).reshape(-1).astype(jnp.float32)
    col = jnp.broadcast_to(jnp.arange(GRID_W)[None, :], (rows, GRID_W)).reshape(-1).astype(jnp.float32)
    axis_dim = HEAD_DIM // 2
    inv = ROPE_THETA ** (-jnp.arange(0, axis_dim, 2, dtype=jnp.float32) / axis_dim)
    ang = jnp.stack([row[:, None] * inv, col[:, None] * inv], axis=1)
    return jnp.cos(ang), jnp.sin(ang)


def _apply_rope(x, cos, sin):
    shp = x.shape
    xr = x.astype(jnp.float32).reshape(shp[:-1] + (2, 2, HEAD_DIM // 4))
    x1, x2 = xr[..., 0, :], xr[..., 1, :]
    cs, sn = cos[:, None], sin[:, None]
    out = jnp.stack([x1 * cs - x2 * sn, x1 * sn + x2 * cs], axis=-2)
    return out.reshape(shp).astype(x.dtype)


def _qkv(h, w_qkv, q_g, k_g):
    bsz, n, _ = h.shape
    qkv = h @ w_qkv
    q, k, v = jnp.split(qkv, [N_Q_HEADS * HEAD_DIM, (N_Q_HEADS + N_KV_HEADS) * HEAD_DIM], axis=-1)
    q = _rmsnorm(q.reshape(bsz, n, N_Q_HEADS, HEAD_DIM), q_g)
    k = _rmsnorm(k.reshape(bsz, n, N_KV_HEADS, HEAD_DIM), k_g)
    v = v.reshape(bsz, n, N_KV_HEADS, HEAD_DIM)
    return q, k, v


def _attend(q, k_parts, v_parts):
    scale = HEAD_DIM ** -0.5
    s = jnp.concatenate([jnp.einsum('bqkgd,bskd->bkgqs', q, kp) for kp in k_parts], axis=-1)
    p = jax.nn.softmax(s.astype(jnp.float32) * scale, axis=-1).astype(v_parts[0].dtype)
    o = None
    off = 0
    for vp in v_parts:
        sz = vp.shape[1]
        term = jnp.einsum('bkgqs,bskd->bqkgd', p[..., off:off + sz], vp)
        o = term if o is None else o + term
        off += sz
    return o


def _attn_mixer(hl, hc, w_qkv, q_g, k_g, w_o, cos, sin, need_ctx):
    bsz, n_l, _ = hl.shape
    n_c = hc.shape[1]
    ql, kl, vl = _qkv(hl, w_qkv, q_g, k_g)
    ql = _apply_rope(ql, cos, sin)
    kl = _apply_rope(kl, cos, sin)
    qc, kc, vc = _qkv(hc, w_qkv, q_g, k_g)
    n_blocks = n_l // Q_BLOCK
    qb = ql.reshape(bsz, n_blocks, Q_BLOCK, N_KV_HEADS, Q_PER_KV, HEAD_DIM).transpose(1, 0, 2, 3, 4, 5)
    ob = lax.map(lambda qblk: _attend(qblk, (kl, kc), (vl, vc)), qb)
    out_l = ob.transpose(1, 0, 2, 3, 4, 5).reshape(bsz, n_l, D_MODEL) @ w_o
    out_c = None
    if need_ctx:
        oc = _attend(qc.reshape(bsz, n_c, N_KV_HEADS, Q_PER_KV, HEAD_DIM), (kc,), (vc,))
        out_c = oc.reshape(bsz, n_c, D_MODEL) @ w_o
    return out_l, out_c


def _swiglu(h, w_gate_up, w_down):
    g, u = jnp.split(h @ w_gate_up, 2, axis=-1)
    return (jax.nn.silu(g) * u) @ w_down


def _moe(h, router_w, router_b, w_gate_up, w_down):
    logits = (h @ router_w).astype(jnp.float32) + router_b.astype(jnp.float32)
    top_v, top_i = lax.top_k(logits, TOP_K)
    top_w = jax.nn.softmax(top_v, axis=-1)
    gates = jnp.sum(jax.nn.one_hot(top_i, N_EXPERTS, dtype=jnp.float32) * top_w[..., None], axis=-2)
    gates = gates.astype(h.dtype)
    y = jnp.zeros_like(h)
    for e in range(N_EXPERTS):
        y = y + gates[..., e:e + 1] * _swiglu(h, w_gate_up[e], w_down[e])
    return y


def setup_inputs(seed: int = 0) -> dict:
    key = jax.random.key(seed)
    ks = jax.random.split(key, 28)
    f32 = jnp.float32
    D, G, P, I = D_MODEL, SSM_GROUPS, SSM_STATE, SSM_GROUP
    NS, NA = N_SSM_LAYERS, N_ATTN_LAYERS

    def nrm(k, shape, scale):
        return jax.random.normal(k, shape, f32) * scale

    n_idx = jnp.arange(P, dtype=f32)
    qkv_width = (N_Q_HEADS + 2 * N_KV_HEADS) * HEAD_DIM
    return {
        'x': nrm(ks[0], (BATCH, SEQ, D), 1.0),
        'c': nrm(ks[1], (BATCH, D), 1.0),
        'ctx': nrm(ks[2], (BATCH, CTX_LEN, D), 1.0),
        'c_ctx': nrm(ks[3], (D,), 1.0),
        'ada_w': nrm(ks[4], (DEPTH, D, N_MOD * D), 0.5 * D ** -0.5),
        'ada_b': nrm(ks[5], (DEPTH, N_MOD * D), 0.02),
        'norm_mix_g': 1.0 + nrm(ks[6], (DEPTH, D), 0.02),
        'norm_ffn_g': 1.0 + nrm(ks[7], (DEPTH, D), 0.02),
        'ssm_a_re': -0.5 + nrm(ks[8], (NS, 2, G, P), 0.01),
        'ssm_a_im': math.pi * n_idx + nrm(ks[9], (NS, 2, G, P), 0.01),
        'ssm_log_dt': jax.random.uniform(ks[10], (NS, 2, G), f32, math.log(SSM_DT_MIN), math.log(SSM_DT_MAX)),
        'ssm_b_re': nrm(ks[11], (NS, 2, G, P, I), (2 * I) ** -0.5),
        'ssm_b_im': nrm(ks[12], (NS, 2, G, P, I), (2 * I) ** -0.5),
        'ssm_c_re': nrm(ks[13], (NS, 2, G, I, P), (4 * P) ** -0.5),
        'ssm_c_im': nrm(ks[14], (NS, 2, G, I, P), (4 * P) ** -0.5),
        'ssm_d': nrm(ks[15], (NS, D), 0.5),
        'ssm_glu_w': nrm(ks[16], (NS, D, 2 * D), D ** -0.5),
        'ssm_glu_b': nrm(ks[17], (NS, 2 * D), 0.02),
        'attn_w_qkv': nrm(ks[18], (NA, D, qkv_width), D ** -0.5),
        'attn_q_g': 1.0 + nrm(ks[19], (NA, HEAD_DIM), 0.02),
        'attn_k_g': 1.0 + nrm(ks[20], (NA, HEAD_DIM), 0.02),
        'attn_w_o': nrm(ks[21], (NA, D, D), D ** -0.5),
        'ffn_w_gate_up': nrm(ks[22], (NS, D, 2 * D_FF), D ** -0.5),
        'ffn_w_down': nrm(ks[23], (NS, D_FF, D), D_FF ** -0.5),
        'moe_router_w': nrm(ks[24], (NA, D, N_EXPERTS), D ** -0.5),
        'moe_router_b': nrm(ks[25], (NA, N_EXPERTS), 0.01),
        'moe_w_gate_up': nrm(ks[26], (NA, N_EXPERTS, D, 2 * D_FF), D ** -0.5),
        'moe_w_down': nrm(ks[27], (NA, N_EXPERTS, D_FF, D), D_FF ** -0.5),
    }


def reference(x, c, ctx, c_ctx, ada_w, ada_b, norm_mix_g, norm_ffn_g,
              ssm_a_re, ssm_a_im, ssm_log_dt, ssm_b_re, ssm_b_im, ssm_c_re, ssm_c_im,
              ssm_d, ssm_glu_w, ssm_glu_b,
              attn_w_qkv, attn_q_g, attn_k_g, attn_w_o,
              ffn_w_gate_up, ffn_w_down,
              moe_router_w, moe_router_b, moe_w_gate_up, moe_w_down):
    n_lat = x.shape[1]
    rows = n_lat // GRID_W
    cos, sin = _axial_rope_tables(rows)
    xl, xc = x, ctx
    for i in range(DEPTH):
        j = i // 2
        last = i == DEPTH - 1
        ml = [m[:, None, :] for m in _modulation(c, ada_w[i], ada_b[i])]
        mc = _modulation(c_ctx, ada_w[i], ada_b[i])
        hl = _modulate(_rmsnorm(xl, norm_mix_g[i]), ml[0], ml[1])
        hc = _modulate(_rmsnorm(xc, norm_mix_g[i]), mc[0], mc[1])
        if i % N_MIXERS == 0:
            yl, yc = _s5_mixer(hl, hc, ssm_a_re[j], ssm_a_im[j], ssm_log_dt[j], ssm_b_re[j], ssm_b_im[j],
                               ssm_c_re[j], ssm_c_im[j], ssm_d[j], ssm_glu_w[j], ssm_glu_b[j],
                               need_ctx=not last)
        else:
            yl, yc = _attn_mixer(hl, hc, attn_w_qkv[j], attn_q_g[j], attn_k_g[j], attn_w_o[j],
                                 cos, sin, need_ctx=not last)
        xl = xl + ml[2] * yl
        if i % 2 == 0:
            ffn = _swiglu
            ffn_args = (ffn_w_gate_up[j], ffn_w_down[j])
        else:
            ffn = _moe
            ffn_args = (moe_router_w[j], moe_router_b[j], moe_w_gate_up[j], moe_w_down[j])
        hl = _modulate(_rmsnorm(xl, norm_ffn_g[i]), ml[3], ml[4])
        xl = xl + ml[5] * ffn(hl, *ffn_args)
        if not last:
            xc = xc + mc[2] * yc
            hc = _modulate(_rmsnorm(xc, norm_ffn_g[i]), mc[3], mc[4])
            xc = xc + mc[5] * ffn(hc, *ffn_args)
    return xl
```

```python
import functools
import math

import jax
import jax.numpy as jnp
from jax import lax
from jax.experimental import pallas as pl
from jax.experimental.pallas import tpu as pltpu

F32 = jnp.float32
BF16 = jnp.bfloat16

D_MODEL = 1024
DEPTH = 4
GRID_W = 64
SSM_GROUP = 16
SSM_GROUPS = D_MODEL // SSM_GROUP
SSM_STATE = 64
HEAD_DIM = 64
N_Q_HEADS = D_MODEL // HEAD_DIM
N_KV_HEADS = 4
Q_PER_KV = N_Q_HEADS // N_KV_HEADS
ROPE_THETA = 10000.0
D_FF = 2816
N_EXPERTS = 8
TOP_K = 2
N_MOD = 6
EPS = 1e-6

LANES = 128
MOD_ROWS = 8

ROW_TILE = 512
FF_TILE = 1408
MOE_TILE = 512
SSM_CHUNK = 16
SSM_PAIR = 2 * SSM_GROUP
N_PAIRS = D_MODEL // SSM_PAIR
SCAN_ROWS = 8
PAIR_W = SSM_CHUNK * SSM_PAIR
Q_TILE = 256
KV_TILE = 1024
VMEM_LIMIT = 56 * 1024 * 1024


def _cparams(sem):
    return pltpu.CompilerParams(dimension_semantics=sem, vmem_limit_bytes=VMEM_LIMIT)


def _row_class(t, tile, bc, n, b):
    r0 = t * tile
    return jnp.where(r0 < bc, b, (r0 - bc) // max(n, 1))


def _mod_spec(piece, tile, bc, n, b):
    return pl.BlockSpec((None, 1, D_MODEL),
                        lambda t, *_: (_row_class(t, tile, bc, n, b), 0, piece))


def _norm_mod(x, g, shift, scale):
    ms = jnp.mean(x * x, axis=-1, keepdims=True)
    return (x * lax.rsqrt(ms + EPS) * g) * (1.0 + scale) + shift


def _mod_kernel(c_ref, w_ref, b_ref, o_ref):
    c = c_ref[...]
    s = (c * jax.nn.sigmoid(c)).astype(BF16)
    o_ref[...] = jnp.dot(s, w_ref[...].astype(BF16), preferred_element_type=F32) + b_ref[...]


def _modulation_table(cvecs, ada_w, ada_b):
    tn = 2048
    width = N_MOD * D_MODEL
    return pl.pallas_call(
        _mod_kernel,
        grid=(DEPTH, width // tn),
        in_specs=[pl.BlockSpec((MOD_ROWS, D_MODEL), lambda i, j: (0, 0)),
                  pl.BlockSpec((None, D_MODEL, tn), lambda i, j: (i, 0, j)),
                  pl.BlockSpec((None, 1, tn), lambda i, j: (i, 0, j))],
        out_specs=pl.BlockSpec((None, MOD_ROWS, tn), lambda i, j: (i, 0, j)),
        out_shape=jax.ShapeDtypeStruct((DEPTH, MOD_ROWS, width), F32),
        compiler_params=_cparams(("parallel", "parallel")),
        name="adaln_table",
    )(cvecs, ada_w, ada_b.reshape(DEPTH, 1, width))


def _prenorm_kernel(x_ref, g_ref, sh_ref, sc_ref, h_ref):
    h_ref[...] = _norm_mod(x_ref[...], g_ref[...], sh_ref[...], sc_ref[...]).astype(BF16)


def _prenorm(x, g, mod, bc, n, b):
    nt = x.shape[0]
    tm = ROW_TILE
    row = pl.BlockSpec((tm, D_MODEL), lambda t: (t, 0))
    return pl.pallas_call(
        _prenorm_kernel,
        grid=(nt // tm,),
        in_specs=[row, pl.BlockSpec((1, D_MODEL), lambda t: (0, 0)),
                  _mod_spec(0, tm, bc, n, b), _mod_spec(1, tm, bc, n, b)],
        out_specs=row,
        out_shape=jax.ShapeDtypeStruct((nt, D_MODEL), BF16),
        compiler_params=_cparams(("parallel",)),
        name="s5_prenorm",
    )(x, g, mod, mod)


def _cmul_add(xr, xi, lr, li, tr, ti):
    return xr + lr * tr - li * ti, xi + lr * ti + li * tr


def _block_scan(er, ei, tab_r, tab_i, s_r, s_i, row, reverse):
    xr, xi = er, ei
    for k in (1, 2, 4):
        if reverse:
            keep = row < SCAN_ROWS - k
            lr, li = tab_r[SCAN_ROWS - k:SCAN_ROWS - k + 1], tab_i[SCAN_ROWS - k:SCAN_ROWS - k + 1]
            shift = SCAN_ROWS - k
        else:
            keep = row >= k
            lr, li = tab_r[k - 1:k], tab_i[k - 1:k]
            shift = k
        tr = jnp.where(keep, pltpu.roll(xr, shift, 0), 0.0)
        ti = jnp.where(keep, pltpu.roll(xi, shift, 0), 0.0)
        xr, xi = _cmul_add(xr, xi, lr, li, tr, ti)
    ar, ai = _cmul_add(xr, xi, tab_r, tab_i, s_r, s_i)
    if reverse:
        edge = row == SCAN_ROWS - 1
        inr = jnp.where(edge, s_r, pltpu.roll(ar, SCAN_ROWS - 1, 0))
        ini = jnp.where(edge, s_i, pltpu.roll(ai, SCAN_ROWS - 1, 0))
        return inr, ini, ar[0:1], ai[0:1]
    edge = row == 0
    inr = jnp.where(edge, s_r, pltpu.roll(ar, 1, 0))
    ini = jnp.where(edge, s_i, pltpu.roll(ai, 1, 0))
    return inr, ini, ar[SCAN_ROWS - 1:SCAN_ROWS], ai[SCAN_ROWS - 1:SCAN_ROWS]


def _s5_core_kernel(u_ref, w3_ref, tz_ref, w2_ref, tab_ref, y_ref, e_scr, s_scr, *, b, n_ctx_chunks, n_chunks):
    u = u_ref[...]
    e_scr[...] = jnp.dot(u, w3_ref[...], preferred_element_type=F32)
    q4 = LANES
    tfr, tfi, trr, tri = tab_ref[0], tab_ref[1], tab_ref[2], tab_ref[3]
    row = lax.broadcasted_iota(jnp.int32, (SCAN_ROWS, q4), 0)
    nb_ctx = n_ctx_chunks // SCAN_ROWS
    nb = n_chunks // SCAN_ROWS

    def body(m, carry):
        mr = jnp.where(m < nb_ctx, nb_ctx - 1 - m, nb - 1 - (m - nb_ctx))
        out = []
        for bi in range(b):
            sfr, sfi, srr, sri = carry[4 * bi:4 * bi + 4]
            rf = pl.ds(pl.multiple_of(bi * n_chunks + m * SCAN_ROWS, SCAN_ROWS), SCAN_ROWS)
            rr = pl.ds(pl.multiple_of(bi * n_chunks + mr * SCAN_ROWS, SCAN_ROWS), SCAN_ROWS)
            inr, ini, sfr, sfi = _block_scan(e_scr[rf, 0:q4], e_scr[rf, q4:2 * q4], tfr, tfi, sfr, sfi, row, False)
            s_scr[rf, 0:q4] = inr
            s_scr[rf, q4:2 * q4] = ini
            inr, ini, srr, sri = _block_scan(e_scr[rr, 2 * q4:3 * q4], e_scr[rr, 3 * q4:4 * q4], trr, tri,
                                             srr, sri, row, True)
            s_scr[rr, 2 * q4:3 * q4] = inr
            s_scr[rr, 3 * q4:4 * q4] = ini
            out += [sfr, sfi, srr, sri]
        return tuple(out)

    z = jnp.zeros((1, q4), F32)
    lax.fori_loop(0, nb, body, (z,) * (4 * b))
    y = jnp.dot(u, tz_ref[...], preferred_element_type=F32)
    y = y + jnp.dot(s_scr[...].astype(BF16), w2_ref[...], preferred_element_type=F32)
    y_ref[...] = y.astype(BF16)


def _s5_core(u, w3, tz, w2, tab, b, n_ctx_chunks, n_chunks):
    rows = u.shape[1]
    mat = pl.BlockSpec((None, PAIR_W, PAIR_W), lambda p: (p, 0, 0))
    seq = pl.BlockSpec((None, rows, PAIR_W), lambda p: (p, 0, 0))
    return pl.pallas_call(
        functools.partial(_s5_core_kernel, b=b, n_ctx_chunks=n_ctx_chunks, n_chunks=n_chunks),
        grid=(N_PAIRS,),
        in_specs=[seq, mat, mat, mat, pl.BlockSpec((None, 4, SCAN_ROWS, LANES), lambda p: (p, 0, 0, 0))],
        out_specs=seq,
        out_shape=jax.ShapeDtypeStruct((N_PAIRS, rows, PAIR_W), BF16),
        scratch_shapes=[pltpu.VMEM((rows, PAIR_W), F32), pltpu.VMEM((rows, PAIR_W), F32)],
        compiler_params=_cparams(("parallel",)),
        name="s5_core",
    )(u, w3, tz, w2, tab)


def _s5_out_kernel(x_ref, y_ref, g_ref, sh_ref, sc_ref, gt_ref, d_ref, w_ref, wb_ref, o_ref):
    x = x_ref[...]
    h = _norm_mod(x, g_ref[...], sh_ref[...], sc_ref[...])
    yy = y_ref[...].astype(F32) + d_ref[...] * h
    z = jax.nn.gelu(yy).astype(BF16)
    ag = jnp.dot(z, w_ref[...], preferred_element_type=F32) + wb_ref[...]
    a = ag[:, :D_MODEL]
    g = ag[:, D_MODEL:]
    o_ref[...] = x + gt_ref[...] * (a * jax.nn.sigmoid(g))


def _s5_out(x, y, g, mod, d, glu_w, glu_b, bc, n, b):
    nt = x.shape[0]
    tm = ROW_TILE
    row = pl.BlockSpec((tm, D_MODEL), lambda t: (t, 0))
    vec = pl.BlockSpec((1, D_MODEL), lambda t: (0, 0))
    return pl.pallas_call(
        _s5_out_kernel,
        grid=(nt // tm,),
        in_specs=[row, row, vec,
                  _mod_spec(0, tm, bc, n, b), _mod_spec(1, tm, bc, n, b), _mod_spec(2, tm, bc, n, b),
                  vec,
                  pl.BlockSpec((D_MODEL, 2 * D_MODEL), lambda t: (0, 0)),
                  pl.BlockSpec((1, 2 * D_MODEL), lambda t: (0, 0))],
        out_specs=row,
        out_shape=jax.ShapeDtypeStruct((nt, D_MODEL), F32),
        compiler_params=_cparams(("parallel",)),
        name="s5_out",
    )(x, y, g, mod, mod, mod, d, glu_w, glu_b)


def _s5_chunk_weights(a_re, a_im, log_dt, b_re, b_im, c_re, c_im):
    hp = lax.Precision.HIGHEST
    L = SSM_CHUNK
    G, P, I = SSM_GROUPS, SSM_STATE, SSM_GROUP
    a_re = a_re.astype(F32)
    a_im = a_im.astype(F32)
    dt = jnp.exp(log_dt.astype(F32))[..., None]
    mag = jnp.exp(a_re * dt)
    lam_re = mag * jnp.cos(a_im * dt)
    lam_im = mag * jnp.sin(a_im * dt)
    den = a_re * a_re + a_im * a_im
    num_re = lam_re - 1.0
    f_re = (num_re * a_re + lam_im * a_im) / den
    f_im = (lam_im * a_re - num_re * a_im) / den
    b_re = b_re.astype(F32)
    b_im = b_im.astype(F32)
    bb_re = f_re[..., None] * b_re - f_im[..., None] * b_im
    bb_im = f_re[..., None] * b_im + f_im[..., None] * b_re
    m = jnp.arange(L + 1, dtype=F32)[:, None, None, None]
    pmag = jnp.exp(m * (a_re * dt)[None])
    pw_re = pmag * jnp.cos(m * (a_im * dt)[None])
    pw_im = pmag * jnp.sin(m * (a_im * dt)[None])
    c_re = c_re.astype(F32)
    c_im = c_im.astype(F32)
    cl_re = c_re[None] * pw_re[:, :, :, None, :] - c_im[None] * pw_im[:, :, :, None, :]
    cl_im = c_re[None] * pw_im[:, :, :, None, :] + c_im[None] * pw_re[:, :, :, None, :]
    lb_re = pw_re[..., None] * bb_re[None] - pw_im[..., None] * bb_im[None]
    lb_im = pw_re[..., None] * bb_im[None] + pw_im[..., None] * bb_re[None]
    kk = (jnp.einsum('mrgop,rgpi->mrgoi', cl_re[:L], bb_re, precision=hp)
          - jnp.einsum('mrgop,rgpi->mrgoi', cl_im[:L], bb_im, precision=hp))
    kidx = jnp.arange(L)[:, None]
    tidx = jnp.arange(L)[None, :]
    lag_f = tidx - kidx
    lag_r = kidx - tidx
    tz_f = jnp.where((lag_f >= 0)[:, :, None, None, None], kk[jnp.clip(lag_f, 0, L - 1), 0], 0.0)
    tz_r = jnp.where((lag_r >= 0)[:, :, None, None, None], kk[jnp.clip(lag_r, 0, L - 1), 1], 0.0)
    tz = jnp.transpose(tz_f + tz_r, (2, 0, 4, 1, 3))
    pf = (L - 1) - jnp.arange(L)
    pr = jnp.arange(L)
    w3 = jnp.stack([lb_re[pf, 0], lb_im[pf, 0], lb_re[pr, 1], lb_im[pr, 1]], axis=0)
    w3 = jnp.transpose(w3, (2, 1, 4, 0, 3))
    qf = jnp.arange(L) + 1
    qr = L - jnp.arange(L)
    w2 = jnp.stack([cl_re[qf, 0], -cl_im[qf, 0], cl_re[qr, 1], -cl_im[qr, 1]], axis=0)
    w2 = jnp.transpose(w2, (2, 0, 4, 1, 3))
    eye2 = jnp.eye(2, dtype=F32)
    tzp = jnp.einsum('PgkitO,gh->PkgithO', tz.reshape(N_PAIRS, 2, L, I, L, I), eye2)
    tzp = tzp.reshape(N_PAIRS, PAIR_W, PAIR_W)
    w3p = jnp.einsum('Pgkicp,gh->Pkgichp', w3.reshape(N_PAIRS, 2, L, I, 4, P), eye2)
    w3p = w3p.reshape(N_PAIRS, PAIR_W, 4 * 2 * P)
    w2p = jnp.einsum('PgcptO,gh->PcgpthO', w2.reshape(N_PAIRS, 2, 4, P, L, I), eye2)
    w2p = w2p.reshape(N_PAIRS, 4 * 2 * P, PAIR_W)
    jf = (jnp.arange(SCAN_ROWS, dtype=F32) + 1.0) * L
    jr = (SCAN_ROWS - jnp.arange(SCAN_ROWS, dtype=F32)) * L
    jj = jnp.stack([jf, jr], axis=0)[:, :, None, None]
    tmag = jnp.exp(jj * (a_re * dt)[:, None])
    t_re = tmag * jnp.cos(jj * (a_im * dt)[:, None])
    t_im = tmag * jnp.sin(jj * (a_im * dt)[:, None])
    tab = jnp.stack([t_re[0], t_im[0], t_re[1], t_im[1]], axis=0)
    tab = jnp.transpose(tab.reshape(4, SCAN_ROWS, N_PAIRS, 2 * P), (2, 0, 1, 3))
    return w3p.astype(BF16), tzp.astype(BF16), w2p.astype(BF16), tab


def _s5_layer(x, g, mod, ssm, bc, n, b):
    (a_re, a_im, log_dt, b_re, b_im, c_re, c_im, d, glu_w, glu_b) = ssm
    c_len = bc // b
    n_ctx_chunks = c_len // SSM_CHUNK
    n_chunks = (c_len + n) // SSM_CHUNK
    h = _prenorm(x, g, mod, bc, n, b)
    hb = jnp.concatenate([h[:bc].reshape(b, c_len, D_MODEL), h[bc:].reshape(b, n, D_MODEL)], axis=1)
    u = hb.reshape(b, n_chunks, SSM_CHUNK, N_PAIRS, SSM_PAIR)
    u = jnp.transpose(u, (3, 0, 1, 2, 4)).reshape(N_PAIRS, b * n_chunks, PAIR_W)
    w3p, tzp, w2p, tab = _s5_chunk_weights(a_re, a_im, log_dt, b_re, b_im, c_re, c_im)
    yp = _s5_core(u, w3p, tzp, w2p, tab, b, n_ctx_chunks, n_chunks)
    yb = yp.reshape(N_PAIRS, b, n_chunks, SSM_CHUNK, SSM_PAIR)
    yb = jnp.transpose(yb, (1, 2, 3, 0, 4)).reshape(b, c_len + n, D_MODEL)
    y = jnp.concatenate([yb[:, :c_len].reshape(bc, D_MODEL), yb[:, c_len:].reshape(b * n, D_MODEL)], axis=0)
    return _s5_out(x, y, g, mod, d.reshape(1, D_MODEL), glu_w.astype(BF16),
                   glu_b.reshape(1, 2 * D_MODEL), bc, n, b)


def _ffn_kernel(x_ref, g_ref, sh_ref, sc_ref, gt_ref, wg_ref, wu_ref, wd_ref, o_ref, h_scr, acc_scr):
    j = pl.program_id(1)

    @pl.when(j == 0)
    def _():
        h_scr[...] = _norm_mod(x_ref[...], g_ref[...], sh_ref[...], sc_ref[...]).astype(BF16)
        acc_scr[...] = jnp.zeros_like(acc_scr)

    h = h_scr[...]
    gate = jnp.dot(h, wg_ref[...], preferred_element_type=F32)
    up = jnp.dot(h, wu_ref[...], preferred_element_type=F32)
    a = (gate * jax.nn.sigmoid(gate) * up).astype(BF16)
    acc_scr[...] += jnp.dot(a, wd_ref[...], preferred_element_type=F32)

    @pl.when(j == pl.num_programs(1) - 1)
    def _():
        o_ref[...] = x_ref[...] + gt_ref[...] * acc_scr[...]


def _ffn(x, g, mod, w_gate_up, w_down, bc, n, b):
    nt = x.shape[0]
    tm = ROW_TILE
    nj = D_FF // FF_TILE
    row = pl.BlockSpec((tm, D_MODEL), lambda t, j: (t, 0))
    return pl.pallas_call(
        _ffn_kernel,
        grid=(nt // tm, nj),
        in_specs=[row, pl.BlockSpec((1, D_MODEL), lambda t, j: (0, 0)),
                  _mod_spec(3, tm, bc, n, b), _mod_spec(4, tm, bc, n, b), _mod_spec(5, tm, bc, n, b),
                  pl.BlockSpec((D_MODEL, FF_TILE), lambda t, j: (0, j)),
                  pl.BlockSpec((D_MODEL, FF_TILE), lambda t, j: (0, nj + j)),
                  pl.BlockSpec((FF_TILE, D_MODEL), lambda t, j: (j, 0))],
        out_specs=row,
        out_shape=jax.ShapeDtypeStruct((nt, D_MODEL), F32),
        scratch_shapes=[pltpu.VMEM((tm, D_MODEL), BF16), pltpu.VMEM((tm, D_MODEL), F32)],
        compiler_params=_cparams(("parallel", "arbitrary")),
        name="ffn_dense",
    )(x, g, mod, mod, mod, w_gate_up, w_gate_up, w_down)


def _head_ms(sq, ones):
    return jnp.dot(sq.astype(BF16), ones, preferred_element_type=F32)


def _rope(v, cos, sin_signed, first_half):
    w = v.shape[-1]
    quarter = HEAD_DIM // 4
    swapped = jnp.where(first_half, pltpu.roll(v, w - quarter, 1), pltpu.roll(v, quarter, 1))
    return v * cos + swapped * sin_signed


def _qkv_kernel(x_ref, g_ref, sh_ref, sc_ref, w_ref, qg_ref, kg_ref, ones_ref, cos_ref, sin_ref,
                q_ref, k_ref, v_ref):
    tm = x_ref.shape[0]
    h = _norm_mod(x_ref[...], g_ref[...], sh_ref[...], sc_ref[...]).astype(BF16)
    qkv = jnp.dot(h, w_ref[...], preferred_element_type=F32)
    dq = N_Q_HEADS * HEAD_DIM
    dk = N_KV_HEADS * HEAD_DIM
    q = qkv[:, :dq]
    k = qkv[:, dq:dq + dk]
    v = qkv[:, dq + dk:]
    ones = ones_ref[...]
    q = q * lax.rsqrt(_head_ms(q * q, ones) + EPS) * qg_ref[...]
    k = k * lax.rsqrt(_head_ms(k * k, ones[:dk, :dk]) + EPS) * kg_ref[...]
    cos = cos_ref[...]
    sin = sin_ref[...]
    lane_q = lax.broadcasted_iota(jnp.int32, (tm, dq), 1)
    lane_k = lax.broadcasted_iota(jnp.int32, (tm, dk), 1)
    half = HEAD_DIM // 2
    quarter = HEAD_DIM // 4
    q = _rope(q, jnp.tile(cos, (1, dq // LANES)), jnp.tile(sin, (1, dq // LANES)), (lane_q % half) < quarter)
    k = _rope(k, jnp.tile(cos, (1, dk // LANES)), jnp.tile(sin, (1, dk // LANES)), (lane_k % half) < quarter)
    q = q * (HEAD_DIM ** -0.5)
    k_ref[...] = k.astype(BF16)
    v_ref[...] = v.astype(BF16)
    lane = lax.broadcasted_iota(jnp.int32, (tm, LANES), 1)
    low = lane < HEAD_DIM
    for c in range(dq // LANES):
        vc = q[:, c * LANES:(c + 1) * LANES]
        vr = pltpu.roll(vc, HEAD_DIM, 1)
        kv_half = (c // 2) % 2
        if kv_half == 0:
            first = jnp.where(low, vc, 0.0)
            second = jnp.where(low, vr, 0.0)
        else:
            first = jnp.where(low, 0.0, vr)
            second = jnp.where(low, 0.0, vc)
        q_ref[:, (2 * c) * LANES:(2 * c + 1) * LANES] = first.astype(BF16)
        q_ref[:, (2 * c + 1) * LANES:(2 * c + 2) * LANES] = second.astype(BF16)


def _qkv(x, g, mod, w_qkv, q_g, k_g, ones, cos, sin, bc, n, b):
    nt = x.shape[0]
    tm = ROW_TILE
    dq = N_Q_HEADS * HEAD_DIM
    dk = N_KV_HEADS * HEAD_DIM
    row = lambda w: pl.BlockSpec((tm, w), lambda t: (t, 0))
    full = lambda r, w: pl.BlockSpec((r, w), lambda t: (0, 0))
    return pl.pallas_call(
        _qkv_kernel,
        grid=(nt // tm,),
        in_specs=[row(D_MODEL), full(1, D_MODEL), _mod_spec(0, tm, bc, n, b), _mod_spec(1, tm, bc, n, b),
                  full(D_MODEL, dq + 2 * dk), full(1, dq), full(1, dk), full(dq, dq),
                  row(LANES), row(LANES)],
        out_specs=[row(2 * dq), row(dk), row(dk)],
        out_shape=[jax.ShapeDtypeStruct((nt, 2 * dq), BF16),
                   jax.ShapeDtypeStruct((nt, dk), BF16),
                   jax.ShapeDtypeStruct((nt, dk), BF16)],
        compiler_params=_cparams(("parallel",)),
        name="attn_qkv",
    )(x, g, mod, mod, w_qkv, q_g, k_g, ones, cos, sin)


def _softmax_step(qs, k, v, m_scr, l_scr, acc_scr):
    s = lax.dot_general(qs, k, (((1,), (1,)), ((), ())), preferred_element_type=F32)
    m_prev = m_scr[...]
    m_new = jnp.maximum(m_prev, jnp.max(s, axis=-1, keepdims=True))
    alpha = jnp.exp(m_prev - m_new)
    p = jnp.exp(s - m_new)
    l_scr[...] = alpha * l_scr[...] + jnp.sum(p, axis=-1, keepdims=True)
    acc_scr[...] = alpha * acc_scr[...] + jnp.dot(p.astype(BF16), v, preferred_element_type=F32)
    m_scr[...] = m_new


def _flash_kernel(*refs, tq, has_latent):
    if has_latent:
        q_ref, kc_ref, vc_ref, kl_ref, vl_ref, o_ref, qs_scr, m_scr, l_scr, acc_scr = refs
        kj = pl.program_id(3)
        last = pl.num_programs(3) - 1
    else:
        q_ref, kc_ref, vc_ref, o_ref, qs_scr, m_scr, l_scr, acc_scr = refs
        kj = 0
        last = 0
    grp = pl.program_id(1)

    def start():
        for a in range(Q_PER_KV):
            qs_scr[a * tq:(a + 1) * tq, :] = q_ref[:, a * LANES:(a + 1) * LANES]
        m_scr[...] = jnp.full_like(m_scr, -jnp.inf)
        l_scr[...] = jnp.zeros_like(l_scr)
        acc_scr[...] = jnp.zeros_like(acc_scr)
        _softmax_step(qs_scr[...], kc_ref[...], vc_ref[...], m_scr, l_scr, acc_scr)

    def finish():
        o = acc_scr[...] / l_scr[...]
        lane = lax.broadcasted_iota(jnp.int32, (tq, LANES), 1)
        low = lane < HEAD_DIM
        upper_half = (grp % 2) == 1
        for c in range(Q_PER_KV // 2):
            a0 = o[(2 * c) * tq:(2 * c + 1) * tq]
            a1 = o[(2 * c + 1) * tq:(2 * c + 2) * tq]
            from_low = jnp.where(low, a0, pltpu.roll(a1, HEAD_DIM, 1))
            from_high = jnp.where(low, pltpu.roll(a0, HEAD_DIM, 1), a1)
            o_ref[:, c * LANES:(c + 1) * LANES] = jnp.where(upper_half, from_high, from_low).astype(BF16)

    if has_latent:
        pl.when(kj == 0)(start)
        _softmax_step(qs_scr[...], kl_ref[...], vl_ref[...], m_scr, l_scr, acc_scr)
        pl.when(kj == last)(finish)
    else:
        start()
        finish()


def _flash_scratch(tq):
    rows = Q_PER_KV * tq
    return [pltpu.VMEM((rows, LANES), BF16), pltpu.VMEM((rows, 1), F32),
            pltpu.VMEM((rows, 1), F32), pltpu.VMEM((rows, LANES), F32)]


def _flash_latent(qz, k, v, bc, n, b):
    c_len = bc // b
    tq = min(Q_TILE, n)
    tk = min(KV_TILE, n)
    gw = Q_PER_KV * LANES
    grid = (b, N_KV_HEADS, n // tq, n // tk)
    q_spec = pl.BlockSpec((tq, gw), lambda bi, g, qi, kj: ((bc + bi * n) // tq + qi, g))
    ctx_spec = pl.BlockSpec((c_len, LANES), lambda bi, g, qi, kj: (bi, g // 2))
    lat_spec = pl.BlockSpec((tk, LANES), lambda bi, g, qi, kj: ((bc + bi * n) // tk + kj, g // 2))
    o_spec = pl.BlockSpec((tq, Q_PER_KV * HEAD_DIM), lambda bi, g, qi, kj: (bi * (n // tq) + qi, g))
    return pl.pallas_call(
        functools.partial(_flash_kernel, tq=tq, has_latent=True),
        grid=grid,
        in_specs=[q_spec, ctx_spec, ctx_spec, lat_spec, lat_spec],
        out_specs=o_spec,
        out_shape=jax.ShapeDtypeStruct((b * n, D_MODEL), BF16),
        scratch_shapes=_flash_scratch(tq),
        compiler_params=_cparams(("parallel", "parallel", "parallel", "arbitrary")),
        name="attn_flash_latent",
    )(qz, k, v, k, v)


def _flash_ctx(qz, k, v, bc, b):
    c_len = bc // b
    gw = Q_PER_KV * LANES
    q_spec = pl.BlockSpec((c_len, gw), lambda bi, g: (bi, g))
    ctx_spec = pl.BlockSpec((c_len, LANES), lambda bi, g: (bi, g // 2))
    o_spec = pl.BlockSpec((c_len, Q_PER_KV * HEAD_DIM), lambda bi, g: (bi, g))
    return pl.pallas_call(
        functools.partial(_flash_kernel, tq=c_len, has_latent=False),
        grid=(b, N_KV_HEADS),
        in_specs=[q_spec, ctx_spec, ctx_spec],
        out_specs=o_spec,
        out_shape=jax.ShapeDtypeStruct((bc, D_MODEL), BF16),
        scratch_shapes=_flash_scratch(c_len),
        compiler_params=_cparams(("parallel", "parallel")),
        name="attn_flash_ctx",
    )(qz, k, v)


def _attn_out_kernel(o_ref, x_ref, wo_ref, gt_ref, g_ref, sh_ref, sc_ref, rw_ref, rb_ref,
                     xo_ref, h_ref, r_ref):
    tm = x_ref.shape[0]
    x = x_ref[...] + gt_ref[...] * jnp.dot(o_ref[...], wo_ref[...], preferred_element_type=F32)
    xo_ref[...] = x
    h = _norm_mod(x, g_ref[...], sh_ref[...], sc_ref[...])
    h_ref[...] = h.astype(BF16)
    logits = jnp.dot(h, rw_ref[...], preferred_element_type=F32, precision=lax.Precision.HIGHEST) + rb_ref[...]
    lane = lax.broadcasted_iota(jnp.int32, (tm, LANES), 1)
    neg = -jnp.inf
    logits = jnp.where(lane < N_EXPERTS, logits, neg)
    m1 = jnp.max(logits, axis=-1, keepdims=True)
    i1 = jnp.min(jnp.where(logits == m1, lane, LANES), axis=-1, keepdims=True)
    rest = jnp.where(lane == i1, neg, logits)
    m2 = jnp.max(rest, axis=-1, keepdims=True)
    i2 = jnp.min(jnp.where(rest == m2, lane, LANES), axis=-1, keepdims=True)
    e = jnp.exp(m2 - m1)
    w1 = 1.0 / (1.0 + e)
    w2 = e / (1.0 + e)
    r = jnp.where(lane == 0, i1.astype(F32), 0.0)
    r = jnp.where(lane == 1, i2.astype(F32), r)
    r = jnp.where(lane == 2, w1, r)
    r = jnp.where(lane == 3, w2, r)
    r_ref[...] = r


def _attn_out(o, x, w_o, g, mod, router_w, router_b, bc, n, b):
    nt = x.shape[0]
    tm = ROW_TILE
    row = lambda w: pl.BlockSpec((tm, w), lambda t: (t, 0))
    full = lambda r, w: pl.BlockSpec((r, w), lambda t: (0, 0))
    return pl.pallas_call(
        _attn_out_kernel,
        grid=(nt // tm,),
        in_specs=[row(D_MODEL), row(D_MODEL), full(D_MODEL, D_MODEL), _mod_spec(2, tm, bc, n, b),
                  full(1, D_MODEL), _mod_spec(3, tm, bc, n, b), _mod_spec(4, tm, bc, n, b),
                  full(D_MODEL, LANES), full(1, LANES)],
        out_specs=[row(D_MODEL), row(D_MODEL), row(LANES)],
        out_shape=[jax.ShapeDtypeStruct((nt, D_MODEL), F32),
                   jax.ShapeDtypeStruct((nt, D_MODEL), BF16),
                   jax.ShapeDtypeStruct((nt, LANES), F32)],
        compiler_params=_cparams(("parallel",)),
        name="attn_out_router",
    )(o, x, w_o, mod, g, mod, mod, router_w, router_b)


def _moe_ffn_kernel(te_ref, tv_ref, x_ref, wg_ref, wu_ref, wd_ref, o_ref, acc_scr):
    i = pl.program_id(0)
    j = pl.program_id(1)
    valid = tv_ref[i] > 0

    @pl.when(j == 0)
    def _():
        acc_scr[...] = jnp.zeros_like(acc_scr)

    @pl.when(valid)
    def _():
        h = x_ref[...]
        gate = jnp.dot(h, wg_ref[...], preferred_element_type=F32)
        up = jnp.dot(h, wu_ref[...], preferred_element_type=F32)
        a = (gate * jax.nn.sigmoid(gate) * up).astype(BF16)
        acc_scr[...] += jnp.dot(a, wd_ref[...], preferred_element_type=F32)

    @pl.when(j == pl.num_programs(1) - 1)
    def _():
        o_ref[...] = acc_scr[...].astype(BF16)


def _moe_ffn(xs, tile_expert, tile_valid, w_gate_up, w_down):
    slots = xs.shape[0]
    tm = MOE_TILE
    nj = D_FF // FF_TILE
    row = pl.BlockSpec((tm, D_MODEL), lambda i, j, te, tv: (i, 0))
    grid_spec = pltpu.PrefetchScalarGridSpec(
        num_scalar_prefetch=2,
        grid=(slots // tm, nj),
        in_specs=[row,
                  pl.BlockSpec((None, D_MODEL, FF_TILE), lambda i, j, te, tv: (te[i], 0, j)),
                  pl.BlockSpec((None, D_MODEL, FF_TILE), lambda i, j, te, tv: (te[i], 0, nj + j)),
                  pl.BlockSpec((None, FF_TILE, D_MODEL), lambda i, j, te, tv: (te[i], j, 0))],
        out_specs=row,
        scratch_shapes=[pltpu.VMEM((tm, D_MODEL), F32)])
    return pl.pallas_call(
        _moe_ffn_kernel,
        grid_spec=grid_spec,
        out_shape=jax.ShapeDtypeStruct((slots, D_MODEL), BF16),
        compiler_params=_cparams(("parallel", "arbitrary")),
        name="moe_ffn",
    )(tile_expert, tile_valid, xs, w_gate_up, w_gate_up, w_down)


def _moe_combine_kernel(x_ref, a_ref, b_ref, r_ref, gt_ref, o_ref):
    tm = x_ref.shape[0]
    lane = lax.broadcasted_iota(jnp.int32, (tm, LANES), 1)
    r = r_ref[...]
    w1 = jnp.sum(jnp.where(lane == 2, r, 0.0), axis=-1, keepdims=True)
    w2 = jnp.sum(jnp.where(lane == 3, r, 0.0), axis=-1, keepdims=True)
    y = w1 * a_ref[...].astype(F32) + w2 * b_ref[...].astype(F32)
    o_ref[...] = x_ref[...] + gt_ref[...] * y


def _moe_combine(x, ya, yb, route, mod, bc, n, b):
    nt = x.shape[0]
    tm = ROW_TILE
    row = lambda w: pl.BlockSpec((tm, w), lambda t: (t, 0))
    return pl.pallas_call(
        _moe_combine_kernel,
        grid=(nt // tm,),
        in_specs=[row(D_MODEL), row(D_MODEL), row(D_MODEL), row(LANES), _mod_spec(5, tm, bc, n, b)],
        out_specs=row(D_MODEL),
        out_shape=jax.ShapeDtypeStruct((nt, D_MODEL), F32),
        compiler_params=_cparams(("parallel",)),
        name="moe_combine",
    )(x, ya, yb, route, mod)


def _moe_layer(x, h, route, mod, w_gate_up, w_down, bc, n, b):
    nt = x.shape[0]
    tm = MOE_TILE
    top_i = route[:, :TOP_K].astype(jnp.int32)
    flat_e = top_i.reshape(-1)
    onehot = (flat_e[:, None] == jnp.arange(N_EXPERTS)[None, :]).astype(jnp.int32)
    counts = jnp.sum(onehot, axis=0)
    rank = jnp.sum((jnp.cumsum(onehot, axis=0) - onehot) * onehot, axis=1)
    padded = ((counts + tm - 1) // tm) * tm
    ends = jnp.cumsum(padded)
    starts = ends - padded
    pos = starts[flat_e] + rank
    n_tiles = (TOP_K * nt) // tm + N_EXPERTS
    slots = n_tiles * tm
    token_of_slot = jnp.zeros((slots,), jnp.int32).at[pos].set(jnp.arange(TOP_K * nt, dtype=jnp.int32) // TOP_K)
    tile_start = jnp.arange(n_tiles, dtype=jnp.int32) * tm
    tile_expert = jnp.sum((tile_start[:, None] >= ends[None, :]).astype(jnp.int32), axis=1)
    tile_valid = (tile_start < ends[-1]).astype(jnp.int32)
    last_expert = jnp.max(jnp.where(counts > 0, jnp.arange(N_EXPERTS), 0))
    tile_expert = jnp.where(tile_valid > 0, tile_expert, last_expert).astype(jnp.int32)
    xs = jnp.take(h, token_of_slot, axis=0)
    ys = _moe_ffn(xs, tile_expert, tile_valid, w_gate_up, w_down)
    pos2 = pos.reshape(nt, TOP_K)
    ya = jnp.take(ys, pos2[:, 0], axis=0)
    yb = jnp.take(ys, pos2[:, 1], axis=0)
    return _moe_combine(x, ya, yb, route, mod, bc, n, b)


def _rope_tables(bc, n):
    rows = n // GRID_W
    row = jnp.broadcast_to(jnp.arange(rows)[:, None], (rows, GRID_W)).reshape(-1).astype(F32)
    col = jnp.broadcast_to(jnp.arange(GRID_W)[None, :], (rows, GRID_W)).reshape(-1).astype(F32)
    axis_dim = HEAD_DIM // 2
    inv = ROPE_THETA ** (-jnp.arange(0, axis_dim, 2, dtype=F32) / axis_dim)
    ar = row[:, None] * inv
    ac = col[:, None] * inv
    cos = jnp.concatenate([jnp.cos(ar), jnp.cos(ar), jnp.cos(ac), jnp.cos(ac)], axis=1)
    sin = jnp.concatenate([-jnp.sin(ar), jnp.sin(ar), -jnp.sin(ac), jnp.sin(ac)], axis=1)
    cos = jnp.tile(cos, (1, LANES // HEAD_DIM))
    sin = jnp.tile(sin, (1, LANES // HEAD_DIM))
    return cos, sin


def kernel(x, c, ctx, c_ctx, ada_w, ada_b, norm_mix_g, norm_ffn_g, ssm_a_re, ssm_a_im, ssm_log_dt,
           ssm_b_re, ssm_b_im, ssm_c_re, ssm_c_im, ssm_d, ssm_glu_w, ssm_glu_b, attn_w_qkv, attn_q_g,
           attn_k_g, attn_w_o, ffn_w_gate_up, ffn_w_down, moe_router_w, moe_router_b, moe_w_gate_up,
           moe_w_down):
    b, n, _ = x.shape
    c_len = ctx.shape[1]
    bc = b * c_len
    assert b + 1 <= MOD_ROWS and bc % ROW_TILE == 0 and n % ROW_TILE == 0
    assert c_len % SSM_CHUNK == 0 and n % SSM_CHUNK == 0 and n % GRID_W == 0

    cvecs = jnp.zeros((MOD_ROWS, D_MODEL), F32).at[:b].set(c).at[b].set(c_ctx)
    mods = _modulation_table(cvecs, ada_w, ada_b).reshape(DEPTH, MOD_ROWS, 1, N_MOD * D_MODEL)

    xs = jnp.concatenate([ctx.reshape(bc, D_MODEL), x.reshape(b * n, D_MODEL)], axis=0)

    cos_l, sin_l = _rope_tables(bc, n)
    cos = jnp.concatenate([jnp.ones((bc, LANES), F32), jnp.tile(cos_l, (b, 1))], axis=0)
    sin = jnp.concatenate([jnp.zeros((bc, LANES), F32), jnp.tile(sin_l, (b, 1))], axis=0)
    dq = N_Q_HEADS * HEAD_DIM
    hid = jnp.arange(dq) // HEAD_DIM
    ones = ((hid[:, None] == hid[None, :]).astype(F32) / HEAD_DIM).astype(BF16)

    for i in range(DEPTH):
        j = i // 2
        last = i == DEPTH - 1
        mod = mods[i]
        g_mix = norm_mix_g[i].reshape(1, D_MODEL)
        g_ffn = norm_ffn_g[i].reshape(1, D_MODEL)
        if i % 2 == 0:
            ssm = (ssm_a_re[j], ssm_a_im[j], ssm_log_dt[j], ssm_b_re[j], ssm_b_im[j], ssm_c_re[j],
                   ssm_c_im[j], ssm_d[j], ssm_glu_w[j], ssm_glu_b[j])
            xs = _s5_layer(xs, g_mix, mod, ssm, bc, n, b)
            xs = _ffn(xs, g_ffn, mod, ffn_w_gate_up[j].astype(BF16), ffn_w_down[j].astype(BF16), bc, n, b)
        else:
            qz, k, v = _qkv(xs, g_mix, mod, attn_w_qkv[j].astype(BF16),
                            jnp.tile(attn_q_g[j], N_Q_HEADS).reshape(1, dq),
                            jnp.tile(attn_k_g[j], N_KV_HEADS).reshape(1, N_KV_HEADS * HEAD_DIM),
                            ones, cos, sin, bc, n, b)
            o = _flash_latent(qz, k, v, bc, n, b)
            if last:
                xs = xs[bc:]
                lay = (0, n, b)
            else:
                o = jnp.concatenate([_flash_ctx(qz, k, v, bc, b), o], axis=0)
                lay = (bc, n, b)
            rw = jnp.zeros((D_MODEL, LANES), F32).at[:, :N_EXPERTS].set(moe_router_w[j])
            rb = jnp.zeros((1, LANES), F32).at[0, :N_EXPERTS].set(moe_router_b[j])
            xs, h, route = _attn_out(o, xs, attn_w_o[j].astype(BF16), g_ffn, mod, rw, rb, *lay)
            xs = _moe_layer(xs, h, route, mod, moe_w_gate_up[j].astype(BF16), moe_w_down[j].astype(BF16), *lay)
    return xs.reshape(b, n, D_MODEL)
```

```python
import functools
import math

import jax
import jax.numpy as jnp
from jax import lax
from jax.experimental import pallas as pl
from jax.experimental.pallas import tpu as pltpu

F32 = jnp.float32
BF16 = jnp.bfloat16

D_MODEL = 1024
DEPTH = 4
GRID_W = 64
SSM_GROUP = 16
SSM_GROUPS = D_MODEL // SSM_GROUP
SSM_STATE = 64
HEAD_DIM = 64
N_Q_HEADS = D_MODEL // HEAD_DIM
N_KV_HEADS = 4
Q_PER_KV = N_Q_HEADS // N_KV_HEADS
ROPE_THETA = 10000.0
D_FF = 2816
N_EXPERTS = 8
TOP_K = 2
N_MOD = 6
EPS = 1e-6

LANES = 128
MOD_ROWS = 8

ROW_TILE = 512
FF_TILE = 1408
MOE_TILE = 512
SSM_CHUNK = 16
SSM_PAIR = 2 * SSM_GROUP
N_PAIRS = D_MODEL // SSM_PAIR
SCAN_ROWS = 8
QUAD = LANES // SSM_PAIR
N_QUADS = 2 * SSM_CHUNK - QUAD
PAIR_W = SSM_CHUNK * SSM_PAIR
Q_TILE = 1024
ATTN_SUB = 64
ATTN_DEPTH = 3
ATTN_KEY_CHUNK = 1024
VMEM_LIMIT = 56 * 1024 * 1024


def _cparams(sem):
    return pltpu.CompilerParams(dimension_semantics=sem, vmem_limit_bytes=VMEM_LIMIT)


def _row_class(t, tile, bc, n, b):
    r0 = t * tile
    return jnp.where(r0 < bc, b, (r0 - bc) // max(n, 1))


def _mod_spec(piece, tile, bc, n, b):
    return pl.BlockSpec((None, 1, D_MODEL),
                        lambda t, *_: (_row_class(t, tile, bc, n, b), 0, piece))


def _norm_mod(x, g, shift, scale):
    ms = jnp.mean(x * x, axis=-1, keepdims=True)
    return (x * lax.rsqrt(ms + EPS) * g) * (1.0 + scale) + shift


def _mod_kernel(c_ref, w_ref, b_ref, o_ref):
    c = c_ref[...]
    s = (c * jax.nn.sigmoid(c)).astype(BF16)
    o_ref[...] = jnp.dot(s, w_ref[...].astype(BF16), preferred_element_type=F32) + b_ref[...]


def _modulation_table(cvecs, ada_w, ada_b):
    tn = 2048
    width = N_MOD * D_MODEL
    return pl.pallas_call(
        _mod_kernel,
        grid=(DEPTH, width // tn),
        in_specs=[pl.BlockSpec((MOD_ROWS, D_MODEL), lambda i, j: (0, 0)),
                  pl.BlockSpec((None, D_MODEL, tn), lambda i, j: (i, 0, j)),
                  pl.BlockSpec((None, 1, tn), lambda i, j: (i, 0, j))],
        out_specs=pl.BlockSpec((None, MOD_ROWS, tn), lambda i, j: (i, 0, j)),
        out_shape=jax.ShapeDtypeStruct((DEPTH, MOD_ROWS, width), F32),
        compiler_params=_cparams(("parallel", "parallel")),
        name="adaln_table",
    )(cvecs, ada_w, ada_b.reshape(DEPTH, 1, width))


def _prenorm_kernel(x_ref, g_ref, sh_ref, sc_ref, h_ref):
    h_ref[...] = _norm_mod(x_ref[...], g_ref[...], sh_ref[...], sc_ref[...]).astype(BF16)


def _prenorm(x, g, mod, bc, n, b):
    nt = x.shape[0]
    tm = ROW_TILE
    row = pl.BlockSpec((tm, D_MODEL), lambda t: (t, 0))
    return pl.pallas_call(
        _prenorm_kernel,
        grid=(nt // tm,),
        in_specs=[row, pl.BlockSpec((1, D_MODEL), lambda t: (0, 0)),
                  _mod_spec(0, tm, bc, n, b), _mod_spec(1, tm, bc, n, b)],
        out_specs=row,
        out_shape=jax.ShapeDtypeStruct((nt, D_MODEL), BF16),
        compiler_params=_cparams(("parallel",)),
        name="s5_prenorm",
    )(x, g, mod, mod)


def _cmul_add(xr, xi, lr, li, tr, ti):
    return xr + lr * tr - li * ti, xi + lr * ti + li * tr


def _block_scan(er, ei, tab_r, tab_i, s_r, s_i, row, reverse):
    xr, xi = er, ei
    for k in (1, 2, 4):
        if reverse:
            keep = row < SCAN_ROWS - k
            lr, li = tab_r[SCAN_ROWS - k:SCAN_ROWS - k + 1], tab_i[SCAN_ROWS - k:SCAN_ROWS - k + 1]
            shift = SCAN_ROWS - k
        else:
            keep = row >= k
            lr, li = tab_r[k - 1:k], tab_i[k - 1:k]
            shift = k
        tr = jnp.where(keep, pltpu.roll(xr, shift, 0), 0.0)
        ti = jnp.where(keep, pltpu.roll(xi, shift, 0), 0.0)
        xr, xi = _cmul_add(xr, xi, lr, li, tr, ti)
    ar, ai = _cmul_add(xr, xi, tab_r, tab_i, s_r, s_i)
    if reverse:
        edge = row == SCAN_ROWS - 1
        inr = jnp.where(edge, s_r, pltpu.roll(ar, SCAN_ROWS - 1, 0))
        ini = jnp.where(edge, s_i, pltpu.roll(ai, SCAN_ROWS - 1, 0))
        return inr, ini, ar[0:1], ai[0:1]
    edge = row == 0
    inr = jnp.where(edge, s_r, pltpu.roll(ar, 1, 0))
    ini = jnp.where(edge, s_i, pltpu.roll(ai, 1, 0))
    return inr, ini, ar[SCAN_ROWS - 1:SCAN_ROWS], ai[SCAN_ROWS - 1:SCAN_ROWS]


def _s5_core_kernel(u_ref, w3_ref, w2t_ref, quad_ref, tab_ref, y_ref, e_scr, s_scr, tz_scr, *,
                    b, n_ctx_chunks, n_chunks):
    for k in range(SSM_CHUNK):
        for q in range(SSM_CHUNK // QUAD):
            tz_scr[k * SSM_PAIR:(k + 1) * SSM_PAIR, q * LANES:(q + 1) * LANES] = (
                quad_ref[QUAD * q - k + SSM_CHUNK - 1])
    u = u_ref[...]
    e_scr[...] = jnp.dot(u, w3_ref[...], preferred_element_type=F32)
    q4 = LANES
    tfr, tfi, trr, tri = tab_ref[0], tab_ref[1], tab_ref[2], tab_ref[3]
    row = lax.broadcasted_iota(jnp.int32, (SCAN_ROWS, q4), 0)
    nb_ctx = n_ctx_chunks // SCAN_ROWS
    nb = n_chunks // SCAN_ROWS

    def body(m, carry):
        mr = jnp.where(m < nb_ctx, nb_ctx - 1 - m, nb - 1 - (m - nb_ctx))
        out = []
        for bi in range(b):
            sfr, sfi, srr, sri = carry[4 * bi:4 * bi + 4]
            rf = pl.ds(pl.multiple_of(bi * n_chunks + m * SCAN_ROWS, SCAN_ROWS), SCAN_ROWS)
            rr = pl.ds(pl.multiple_of(bi * n_chunks + mr * SCAN_ROWS, SCAN_ROWS), SCAN_ROWS)
            inr, ini, sfr, sfi = _block_scan(e_scr[rf, 0:q4], e_scr[rf, q4:2 * q4], tfr, tfi, sfr, sfi, row, False)
            s_scr[rf, 0:q4] = inr
            s_scr[rf, q4:2 * q4] = ini
            inr, ini, srr, sri = _block_scan(e_scr[rr, 2 * q4:3 * q4], e_scr[rr, 3 * q4:4 * q4], trr, tri,
                                             srr, sri, row, True)
            s_scr[rr, 2 * q4:3 * q4] = inr
            s_scr[rr, 3 * q4:4 * q4] = ini
            out += [sfr, sfi, srr, sri]
        return tuple(out)

    z = jnp.zeros((1, q4), F32)
    lax.fori_loop(0, nb, body, (z,) * (4 * b))
    y = jnp.dot(u, tz_scr[...], preferred_element_type=F32)
    y = y + lax.dot_general(s_scr[...].astype(BF16), w2t_ref[...], (((1,), (1,)), ((), ())),
                            preferred_element_type=F32)
    y_ref[...] = y.astype(BF16)


def _s5_core(u, w3, w2t, quads, tab, b, n_ctx_chunks, n_chunks):
    rows = u.shape[1]
    mat = pl.BlockSpec((None, PAIR_W, PAIR_W), lambda p: (p, 0, 0))
    seq = pl.BlockSpec((None, rows, PAIR_W), lambda p: (p, 0, 0))
    return pl.pallas_call(
        functools.partial(_s5_core_kernel, b=b, n_ctx_chunks=n_ctx_chunks, n_chunks=n_chunks),
        grid=(N_PAIRS,),
        in_specs=[seq, mat, mat,
                  pl.BlockSpec((None, N_QUADS, SSM_PAIR, LANES), lambda p: (p, 0, 0, 0)),
                  pl.BlockSpec((None, 4, SCAN_ROWS, LANES), lambda p: (p, 0, 0, 0))],
        out_specs=seq,
        out_shape=jax.ShapeDtypeStruct((N_PAIRS, rows, PAIR_W), BF16),
        scratch_shapes=[pltpu.VMEM((rows, PAIR_W), F32), pltpu.VMEM((rows, PAIR_W), F32),
                        pltpu.VMEM((PAIR_W, PAIR_W), BF16)],
        compiler_params=_cparams(("parallel",)),
        name="s5_core",
    )(u, w3, w2t, quads, tab)


def _s5_out_kernel(x_ref, y_ref, g_ref, sh_ref, sc_ref, gt_ref, d_ref, w_ref, wb_ref, o_ref):
    x = x_ref[...]
    h = _norm_mod(x, g_ref[...], sh_ref[...], sc_ref[...])
    yy = y_ref[...].astype(F32) + d_ref[...] * h
    z = jax.nn.gelu(yy).astype(BF16)
    ag = jnp.dot(z, w_ref[...], preferred_element_type=F32) + wb_ref[...]
    a = ag[:, :D_MODEL]
    g = ag[:, D_MODEL:]
    o_ref[...] = x + gt_ref[...] * (a * jax.nn.sigmoid(g))


def _s5_out(x, y, g, mod, d, glu_w, glu_b, bc, n, b):
    nt = x.shape[0]
    tm = ROW_TILE
    row = pl.BlockSpec((tm, D_MODEL), lambda t: (t, 0))
    vec = pl.BlockSpec((1, D_MODEL), lambda t: (0, 0))
    return pl.pallas_call(
        _s5_out_kernel,
        grid=(nt // tm,),
        in_specs=[row, row, vec,
                  _mod_spec(0, tm, bc, n, b), _mod_spec(1, tm, bc, n, b), _mod_spec(2, tm, bc, n, b),
                  vec,
                  pl.BlockSpec((D_MODEL, 2 * D_MODEL), lambda t: (0, 0)),
                  pl.BlockSpec((1, 2 * D_MODEL), lambda t: (0, 0))],
        out_specs=row,
        out_shape=jax.ShapeDtypeStruct((nt, D_MODEL), F32),
        compiler_params=_cparams(("parallel",)),
        name="s5_out",
    )(x, y, g, mod, mod, mod, d, glu_w, glu_b)


def _pair_lanes(x):
    z = jnp.zeros_like(x[..., 0, :, :])
    top = jnp.concatenate([x[..., 0, :, :], z], axis=-1)
    bot = jnp.concatenate([z, x[..., 1, :, :]], axis=-1)
    return jnp.stack([top, bot], axis=-3)


def _s5_chunk_weights(a_re, a_im, log_dt, b_re, b_im, c_re, c_im):
    hp = lax.Precision.HIGHEST
    L = SSM_CHUNK
    G, P, I = SSM_GROUPS, SSM_STATE, SSM_GROUP
    a_re = a_re.astype(F32)
    a_im = a_im.astype(F32)
    dt = jnp.exp(log_dt.astype(F32))[..., None]
    mag = jnp.exp(a_re * dt)
    lam_re = mag * jnp.cos(a_im * dt)
    lam_im = mag * jnp.sin(a_im * dt)
    den = a_re * a_re + a_im * a_im
    num_re = lam_re - 1.0
    f_re = (num_re * a_re + lam_im * a_im) / den
    f_im = (lam_im * a_re - num_re * a_im) / den
    b_re = b_re.astype(F32)
    b_im = b_im.astype(F32)
    bb_re = f_re[..., None] * b_re - f_im[..., None] * b_im
    bb_im = f_re[..., None] * b_im + f_im[..., None] * b_re
    m = jnp.arange(L + 1, dtype=F32)[:, None, None, None]
    pmag = jnp.exp(m * (a_re * dt)[None])
    pw_re = pmag * jnp.cos(m * (a_im * dt)[None])
    pw_im = pmag * jnp.sin(m * (a_im * dt)[None])
    c_re = c_re.astype(F32)
    c_im = c_im.astype(F32)
    cl_re = c_re[None] * pw_re[:, :, :, None, :] - c_im[None] * pw_im[:, :, :, None, :]
    cl_im = c_re[None] * pw_im[:, :, :, None, :] + c_im[None] * pw_re[:, :, :, None, :]
    bt_re = jnp.swapaxes(bb_re, -1, -2)
    bt_im = jnp.swapaxes(bb_im, -1, -2)
    lb_re = pw_re[:, :, :, None, :] * bt_re[None] - pw_im[:, :, :, None, :] * bt_im[None]
    lb_im = pw_re[:, :, :, None, :] * bt_im[None] + pw_im[:, :, :, None, :] * bt_re[None]

    def pair_rows(parts):
        cols = []
        for x in parts:
            x = x.reshape(L, N_PAIRS, 2, x.shape[-2], P)
            cols.append(_pair_lanes(jnp.transpose(x, (1, 0, 2, 3, 4))))
        return jnp.concatenate(cols, axis=-1).reshape(N_PAIRS, PAIR_W, 4 * 2 * P)

    pf = (L - 1) - jnp.arange(L)
    pr = jnp.arange(L)
    w3 = pair_rows([lb_re[pf, 0], lb_im[pf, 0], lb_re[pr, 1], lb_im[pr, 1]])
    qf = jnp.arange(L) + 1
    qr = L - jnp.arange(L)
    w2t = pair_rows([cl_re[qf, 0], -cl_im[qf, 0], cl_re[qr, 1], -cl_im[qr, 1]])
    kk = (jnp.einsum('mrgop,rgpi->mrgio', cl_re[:L], bb_re, precision=hp)
          - jnp.einsum('mrgop,rgpi->mrgio', cl_im[:L], bb_im, precision=hp))
    kp = _pair_lanes(kk.reshape(L, 2, N_PAIRS, 2, I, I)).reshape(L, 2, N_PAIRS, SSM_PAIR, SSM_PAIR)
    lags = jnp.concatenate([kp[:0:-1, 1], kp[:1, 0] + kp[:1, 1], kp[1:, 0]], axis=0)
    quads = jnp.concatenate([lags[q:q + N_QUADS] for q in range(QUAD)], axis=-1)
    quads = jnp.transpose(quads, (1, 0, 2, 3))
    jf = (jnp.arange(SCAN_ROWS, dtype=F32) + 1.0) * L
    jr = (SCAN_ROWS - jnp.arange(SCAN_ROWS, dtype=F32)) * L
    jj = jnp.stack([jf, jr], axis=0)[:, :, None, None]
    tmag = jnp.exp(jj * (a_re * dt)[:, None])
    t_re = tmag * jnp.cos(jj * (a_im * dt)[:, None])
    t_im = tmag * jnp.sin(jj * (a_im * dt)[:, None])
    tab = jnp.stack([t_re[0], t_im[0], t_re[1], t_im[1]], axis=0)
    tab = jnp.transpose(tab.reshape(4, SCAN_ROWS, N_PAIRS, 2 * P), (2, 0, 1, 3))
    return w3.astype(BF16), w2t.astype(BF16), quads.astype(BF16), tab


def _s5_layer(x, g, mod, ssm, bc, n, b):
    (a_re, a_im, log_dt, b_re, b_im, c_re, c_im, d, glu_w, glu_b) = ssm
    c_len = bc // b
    n_ctx_chunks = c_len // SSM_CHUNK
    n_chunks = (c_len + n) // SSM_CHUNK
    h = _prenorm(x, g, mod, bc, n, b)
    hb = jnp.concatenate([h[:bc].reshape(b, c_len, D_MODEL), h[bc:].reshape(b, n, D_MODEL)], axis=1)
    u = hb.reshape(b, n_chunks, SSM_CHUNK, N_PAIRS, SSM_PAIR)
    u = jnp.transpose(u, (3, 0, 1, 2, 4)).reshape(N_PAIRS, b * n_chunks, PAIR_W)
    w3, w2t, quads, tab = _s5_chunk_weights(a_re, a_im, log_dt, b_re, b_im, c_re, c_im)
    yp = _s5_core(u, w3, w2t, quads, tab, b, n_ctx_chunks, n_chunks)
    yb = yp.reshape(N_PAIRS, b, n_chunks, SSM_CHUNK, SSM_PAIR)
    yb = jnp.transpose(yb, (1, 2, 3, 0, 4)).reshape(b, c_len + n, D_MODEL)
    y = jnp.concatenate([yb[:, :c_len].reshape(bc, D_MODEL), yb[:, c_len:].reshape(b * n, D_MODEL)], axis=0)
    return _s5_out(x, y, g, mod, d.reshape(1, D_MODEL), glu_w.astype(BF16),
                   glu_b.reshape(1, 2 * D_MODEL), bc, n, b)


def _ffn_kernel(x_ref, g_ref, sh_ref, sc_ref, gt_ref, wg_ref, wu_ref, wd_ref, o_ref, h_scr, acc_scr):
    j = pl.program_id(1)

    @pl.when(j == 0)
    def _():
        h_scr[...] = _norm_mod(x_ref[...], g_ref[...], sh_ref[...], sc_ref[...]).astype(BF16)
        acc_scr[...] = jnp.zeros_like(acc_scr)

    h = h_scr[...]
    gate = jnp.dot(h, wg_ref[...], preferred_element_type=F32)
    up = jnp.dot(h, wu_ref[...], preferred_element_type=F32)
    a = (gate * jax.nn.sigmoid(gate) * up).astype(BF16)
    acc_scr[...] += jnp.dot(a, wd_ref[...], preferred_element_type=F32)

    @pl.when(j == pl.num_programs(1) - 1)
    def _():
        o_ref[...] = x_ref[...] + gt_ref[...] * acc_scr[...]


def _ffn(x, g, mod, w_gate_up, w_down, bc, n, b):
    nt = x.shape[0]
    tm = ROW_TILE
    nj = D_FF // FF_TILE
    row = pl.BlockSpec((tm, D_MODEL), lambda t, j: (t, 0))
    return pl.pallas_call(
        _ffn_kernel,
        grid=(nt // tm, nj),
        in_specs=[row, pl.BlockSpec((1, D_MODEL), lambda t, j: (0, 0)),
                  _mod_spec(3, tm, bc, n, b), _mod_spec(4, tm, bc, n, b), _mod_spec(5, tm, bc, n, b),
                  pl.BlockSpec((D_MODEL, FF_TILE), lambda t, j: (0, j)),
                  pl.BlockSpec((D_MODEL, FF_TILE), lambda t, j: (0, nj + j)),
                  pl.BlockSpec((FF_TILE, D_MODEL), lambda t, j: (j, 0))],
        out_specs=row,
        out_shape=jax.ShapeDtypeStruct((nt, D_MODEL), F32),
        scratch_shapes=[pltpu.VMEM((tm, D_MODEL), BF16), pltpu.VMEM((tm, D_MODEL), F32)],
        compiler_params=_cparams(("parallel", "arbitrary")),
        name="ffn_dense",
    )(x, g, mod, mod, mod, w_gate_up, w_gate_up, w_down)


def _head_ms(sq, ones):
    return jnp.dot(sq.astype(BF16), ones, preferred_element_type=F32)


def _rope(v, cos, sin_signed, first_half):
    w = v.shape[-1]
    quarter = HEAD_DIM // 4
    swapped = jnp.where(first_half, pltpu.roll(v, w - quarter, 1), pltpu.roll(v, quarter, 1))
    return v * cos + swapped * sin_signed


def _qkv_kernel(x_ref, g_ref, sh_ref, sc_ref, w_ref, qg_ref, kg_ref, ones_ref, cos_ref, sin_ref,
                q_ref, k_ref, v_ref):
    tm = x_ref.shape[0]
    h = _norm_mod(x_ref[...], g_ref[...], sh_ref[...], sc_ref[...]).astype(BF16)
    qkv = jnp.dot(h, w_ref[...], preferred_element_type=F32)
    dq = N_Q_HEADS * HEAD_DIM
    dk = N_KV_HEADS * HEAD_DIM
    q = qkv[:, :dq]
    k = qkv[:, dq:dq + dk]
    v = qkv[:, dq + dk:]
    ones = ones_ref[...]
    q = q * lax.rsqrt(_head_ms(q * q, ones) + EPS) * qg_ref[...]
    k = k * lax.rsqrt(_head_ms(k * k, ones[:dk, :dk]) + EPS) * kg_ref[...]
    cos = cos_ref[...]
    sin = sin_ref[...]
    lane_q = lax.broadcasted_iota(jnp.int32, (tm, dq), 1)
    lane_k = lax.broadcasted_iota(jnp.int32, (tm, dk), 1)
    half = HEAD_DIM // 2
    quarter = HEAD_DIM // 4
    q = _rope(q, jnp.tile(cos, (1, dq // LANES)), jnp.tile(sin, (1, dq // LANES)), (lane_q % half) < quarter)
    k = _rope(k, jnp.tile(cos, (1, dk // LANES)), jnp.tile(sin, (1, dk // LANES)), (lane_k % half) < quarter)
    q = q * (HEAD_DIM ** -0.5)
    k_ref[...] = k.astype(BF16)
    v_ref[...] = v.astype(BF16)
    lane = lax.broadcasted_iota(jnp.int32, (tm, LANES), 1)
    low = lane < HEAD_DIM
    for c in range(dq // LANES):
        vc = q[:, c * LANES:(c + 1) * LANES]
        vr = pltpu.roll(vc, HEAD_DIM, 1)
        kv_half = (c // 2) % 2
        if kv_half == 0:
            first = jnp.where(low, vc, 0.0)
            second = jnp.where(low, vr, 0.0)
        else:
            first = jnp.where(low, 0.0, vr)
            second = jnp.where(low, 0.0, vc)
        q_ref[:, (2 * c) * LANES:(2 * c + 1) * LANES] = first.astype(BF16)
        q_ref[:, (2 * c + 1) * LANES:(2 * c + 2) * LANES] = second.astype(BF16)


def _qkv(x, g, mod, w_qkv, q_g, k_g, ones, cos, sin, bc, n, b):
    nt = x.shape[0]
    tm = ROW_TILE
    dq = N_Q_HEADS * HEAD_DIM
    dk = N_KV_HEADS * HEAD_DIM
    row = lambda w: pl.BlockSpec((tm, w), lambda t: (t, 0))
    full = lambda r, w: pl.BlockSpec((r, w), lambda t: (0, 0))
    return pl.pallas_call(
        _qkv_kernel,
        grid=(nt // tm,),
        in_specs=[row(D_MODEL), full(1, D_MODEL), _mod_spec(0, tm, bc, n, b), _mod_spec(1, tm, bc, n, b),
                  full(D_MODEL, dq + 2 * dk), full(1, dq), full(1, dk), full(dq, dq),
                  row(LANES), row(LANES)],
        out_specs=[row(2 * dq), row(dk), row(dk)],
        out_shape=[jax.ShapeDtypeStruct((nt, 2 * dq), BF16),
                   jax.ShapeDtypeStruct((nt, dk), BF16),
                   jax.ShapeDtypeStruct((nt, dk), BF16)],
        compiler_params=_cparams(("parallel",)),
        name="attn_qkv",
    )(x, g, mod, mod, w_qkv, q_g, k_g, ones, cos, sin)


def _attn_kernel(*refs, sub, n_sub, n_lat):
    if n_lat:
        q_ref, kc_ref, vc_ref, kl_ref, vl_ref, o_ref = refs[:6]
    else:
        q_ref, kc_ref, vc_ref, o_ref = refs[:4]
    s_bufs = refs[-2 * ATTN_DEPTH:-ATTN_DEPTH]
    m_bufs = refs[-ATTN_DEPTH:]
    grp = pl.program_id(1)
    nt_dims = (((1,), (1,)), ((), ()))
    c_len = kc_ref.shape[0]
    parts = [(kc_ref, vc_ref, 0, 0, c_len)]
    if n_lat:
        kc = min(ATTN_KEY_CHUNK, n_lat)
        parts += [(kl_ref, vl_ref, c0, c_len + c0, kc) for c0 in range(0, n_lat, kc)]
    lane = lax.broadcasted_iota(jnp.int32, (sub, LANES), 1)
    low = lane < HEAD_DIM
    upper_half = (grp % 2) == 1

    def scores(i, s_scr, m_scr):
        r0 = pl.multiple_of(i * sub, sub)
        qs = jnp.concatenate([q_ref[pl.ds(r0, sub), a * LANES:(a + 1) * LANES] for a in range(Q_PER_KV)],
                             axis=0)
        m = None
        for kr, _, k0, off, sz in parts:
            s = lax.dot_general(qs, kr[k0:k0 + sz, :], nt_dims, preferred_element_type=F32)
            s_scr[:, off:off + sz] = s
            mx = s.max(axis=-1, keepdims=True)
            m = mx if m is None else jnp.maximum(m, mx)
        m_scr[...] = m

    def output(i, s_scr, m_scr):
        r0 = pl.multiple_of(i * sub, sub)
        m = m_scr[...]
        l = jnp.zeros_like(m)
        acc = jnp.zeros((Q_PER_KV * sub, LANES), F32)
        for _, vr, k0, off, sz in parts:
            p = jnp.exp(s_scr[:, off:off + sz] - m)
            l = l + p.sum(axis=-1, keepdims=True)
            acc = acc + jnp.dot(p.astype(BF16), vr[k0:k0 + sz, :], preferred_element_type=F32)
        o = acc / l
        for c in range(Q_PER_KV // 2):
            a0 = o[(2 * c) * sub:(2 * c + 1) * sub]
            a1 = o[(2 * c + 1) * sub:(2 * c + 2) * sub]
            from_low = jnp.where(low, a0, pltpu.roll(a1, HEAD_DIM, 1))
            from_high = jnp.where(low, pltpu.roll(a0, HEAD_DIM, 1), a1)
            o_ref[pl.ds(r0, sub), c * LANES:(c + 1) * LANES] = (
                jnp.where(upper_half, from_high, from_low).astype(BF16))

    depth = ATTN_DEPTH
    ahead = depth - 1

    def stage(i, r):
        output(i, s_bufs[r], m_bufs[r])
        nxt = (r + ahead) % depth
        scores(i + ahead, s_bufs[nxt], m_bufs[nxt])

    for t in range(min(ahead, n_sub)):
        scores(t, s_bufs[t], m_bufs[t])
    n_loop = max(n_sub - ahead, 0) // depth

    def body(j, carry):
        for r in range(depth):
            stage(j * depth + r, r)
        return carry

    if n_loop:
        lax.fori_loop(0, n_loop, body, 0)
    for i in range(n_loop * depth, n_sub):
        if i + ahead < n_sub:
            stage(i, i % depth)
        else:
            output(i, s_bufs[i % depth], m_bufs[i % depth])


def _attn_call(name, tq, n_lat, c_len, grid, in_specs, out_spec, out_rows, args):
    rows = Q_PER_KV * ATTN_SUB
    assert tq % ATTN_SUB == 0
    return pl.pallas_call(
        functools.partial(_attn_kernel, sub=ATTN_SUB, n_sub=tq // ATTN_SUB, n_lat=n_lat),
        grid=grid,
        in_specs=in_specs,
        out_specs=out_spec,
        out_shape=jax.ShapeDtypeStruct((out_rows, D_MODEL), BF16),
        scratch_shapes=([pltpu.VMEM((rows, c_len + n_lat), F32)] * ATTN_DEPTH
                        + [pltpu.VMEM((rows, 1), F32)] * ATTN_DEPTH),
        compiler_params=_cparams(("parallel", "parallel", "parallel")),
        name=name,
    )(*args)


def _attn_latent(qz, k, v, bc, n, b):
    c_len = bc // b
    tq = min(Q_TILE, n)
    assert bc % tq == 0 and n % tq == 0
    gw = Q_PER_KV * LANES
    k_lat, v_lat = k[bc:], v[bc:]
    q_spec = pl.BlockSpec((tq, gw), lambda bi, g, qi: ((bc + bi * n) // tq + qi, g))
    ctx_spec = pl.BlockSpec((c_len, LANES), lambda bi, g, qi: (bi, g // 2))
    lat_spec = pl.BlockSpec((n, LANES), lambda bi, g, qi: (bi, g // 2))
    o_spec = pl.BlockSpec((tq, Q_PER_KV * HEAD_DIM), lambda bi, g, qi: (bi * (n // tq) + qi, g))
    return _attn_call("attn_latent", tq, n, c_len, (b, N_KV_HEADS, n // tq),
                      [q_spec, ctx_spec, ctx_spec, lat_spec, lat_spec], o_spec, b * n,
                      (qz, k, v, k_lat, v_lat))


def _attn_ctx(qz, k, v, bc, b):
    c_len = bc // b
    gw = Q_PER_KV * LANES
    q_spec = pl.BlockSpec((c_len, gw), lambda bi, g, qi: (bi, g))
    ctx_spec = pl.BlockSpec((c_len, LANES), lambda bi, g, qi: (bi, g // 2))
    o_spec = pl.BlockSpec((c_len, Q_PER_KV * HEAD_DIM), lambda bi, g, qi: (bi, g))
    return _attn_call("attn_ctx", c_len, 0, c_len, (b, N_KV_HEADS, 1),
                      [q_spec, ctx_spec, ctx_spec], o_spec, bc, (qz, k, v))


def _attn_out_kernel(o_ref, x_ref, wo_ref, gt_ref, g_ref, sh_ref, sc_ref, rw_ref, rb_ref,
                     xo_ref, h_ref, r_ref):
    tm = x_ref.shape[0]
    x = x_ref[...] + gt_ref[...] * jnp.dot(o_ref[...], wo_ref[...], preferred_element_type=F32)
    xo_ref[...] = x
    h = _norm_mod(x, g_ref[...], sh_ref[...], sc_ref[...])
    h_ref[...] = h
    logits = jnp.dot(h, rw_ref[...], preferred_element_type=F32, precision=lax.Precision.HIGHEST) + rb_ref[...]
    lane = lax.broadcasted_iota(jnp.int32, (tm, LANES), 1)
    neg = -jnp.inf
    logits = jnp.where(lane < N_EXPERTS, logits, neg)
    m1 = jnp.max(logits, axis=-1, keepdims=True)
    i1 = jnp.min(jnp.where(logits == m1, lane, LANES), axis=-1, keepdims=True)
    rest = jnp.where(lane == i1, neg, logits)
    m2 = jnp.max(rest, axis=-1, keepdims=True)
    i2 = jnp.min(jnp.where(rest == m2, lane, LANES), axis=-1, keepdims=True)
    e = jnp.exp(m2 - m1)
    w1 = 1.0 / (1.0 + e)
    w2 = e / (1.0 + e)
    r = jnp.where(lane == 0, i1.astype(F32), 0.0)
    r = jnp.where(lane == 1, i2.astype(F32), r)
    r = jnp.where(lane == 2, w1, r)
    r = jnp.where(lane == 3, w2, r)
    r_ref[...] = r


def _attn_out(o, x, w_o, g, mod, router_w, router_b, bc, n, b):
    nt = x.shape[0]
    tm = ROW_TILE
    row = lambda w: pl.BlockSpec((tm, w), lambda t: (t, 0))
    full = lambda r, w: pl.BlockSpec((r, w), lambda t: (0, 0))
    return pl.pallas_call(
        _attn_out_kernel,
        grid=(nt // tm,),
        in_specs=[row(D_MODEL), row(D_MODEL), full(D_MODEL, D_MODEL), _mod_spec(2, tm, bc, n, b),
                  full(1, D_MODEL), _mod_spec(3, tm, bc, n, b), _mod_spec(4, tm, bc, n, b),
                  full(D_MODEL, LANES), full(1, LANES)],
        out_specs=[row(D_MODEL), row(D_MODEL), row(LANES)],
        out_shape=[jax.ShapeDtypeStruct((nt, D_MODEL), F32),
                   jax.ShapeDtypeStruct((nt, D_MODEL), F32),
                   jax.ShapeDtypeStruct((nt, LANES), F32)],
        compiler_params=_cparams(("parallel",)),
        name="attn_out_router",
    )(o, x, w_o, mod, g, mod, mod, router_w, router_b)


def _moe_ffn_kernel(te_ref, tv_ref, x_ref, wg_ref, wu_ref, wd_ref, o_ref, h_scr, acc_scr):
    i = pl.program_id(0)
    j = pl.program_id(1)
    valid = tv_ref[i] > 0

    @pl.when(j == 0)
    def _():
        h_scr[...] = x_ref[...].astype(BF16)
        acc_scr[...] = jnp.zeros_like(acc_scr)

    @pl.when(valid)
    def _():
        h = h_scr[...]
        gate = jnp.dot(h, wg_ref[...], preferred_element_type=F32)
        up = jnp.dot(h, wu_ref[...], preferred_element_type=F32)
        a = (gate * jax.nn.sigmoid(gate) * up).astype(BF16)
        acc_scr[...] += jnp.dot(a, wd_ref[...], preferred_element_type=F32)

    @pl.when(j == pl.num_programs(1) - 1)
    def _():
        o_ref[...] = acc_scr[...].astype(BF16)


def _moe_ffn(xs, tile_expert, tile_valid, w_gate_up, w_down):
    slots = xs.shape[0]
    tm = MOE_TILE
    nj = D_FF // FF_TILE
    row = pl.BlockSpec((tm, D_MODEL), lambda i, j, te, tv: (i, 0))
    grid_spec = pltpu.PrefetchScalarGridSpec(
        num_scalar_prefetch=2,
        grid=(slots // tm, nj),
        in_specs=[row,
                  pl.BlockSpec((None, D_MODEL, FF_TILE), lambda i, j, te, tv: (te[i], 0, j)),
                  pl.BlockSpec((None, D_MODEL, FF_TILE), lambda i, j, te, tv: (te[i], 0, nj + j)),
                  pl.BlockSpec((None, FF_TILE, D_MODEL), lambda i, j, te, tv: (te[i], j, 0))],
        out_specs=row,
        scratch_shapes=[pltpu.VMEM((tm, D_MODEL), BF16), pltpu.VMEM((tm, D_MODEL), F32)])
    return pl.pallas_call(
        _moe_ffn_kernel,
        grid_spec=grid_spec,
        out_shape=jax.ShapeDtypeStruct((slots, D_MODEL), BF16),
        compiler_params=_cparams(("parallel", "arbitrary")),
        name="moe_ffn",
    )(tile_expert, tile_valid, xs, w_gate_up, w_gate_up, w_down)


def _moe_combine_kernel(x_ref, a_ref, b_ref, r_ref, gt_ref, o_ref):
    tm = x_ref.shape[0]
    lane = lax.broadcasted_iota(jnp.int32, (tm, LANES), 1)
    r = r_ref[...]
    w1 = jnp.sum(jnp.where(lane == 2, r, 0.0), axis=-1, keepdims=True)
    w2 = jnp.sum(jnp.where(lane == 3, r, 0.0), axis=-1, keepdims=True)
    y = w1 * a_ref[...].astype(F32) + w2 * b_ref[...].astype(F32)
    o_ref[...] = x_ref[...] + gt_ref[...] * y


def _moe_combine(x, ya, yb, route, mod, bc, n, b):
    nt = x.shape[0]
    tm = ROW_TILE
    row = lambda w: pl.BlockSpec((tm, w), lambda t: (t, 0))
    return pl.pallas_call(
        _moe_combine_kernel,
        grid=(nt // tm,),
        in_specs=[row(D_MODEL), row(D_MODEL), row(D_MODEL), row(LANES), _mod_spec(5, tm, bc, n, b)],
        out_specs=row(D_MODEL),
        out_shape=jax.ShapeDtypeStruct((nt, D_MODEL), F32),
        compiler_params=_cparams(("parallel",)),
        name="moe_combine",
    )(x, ya, yb, route, mod)


def _moe_layer(x, h, route, mod, w_gate_up, w_down, bc, n, b):
    nt = x.shape[0]
    tm = MOE_TILE
    top_i = route[:, :TOP_K].astype(jnp.int32)
    flat_e = top_i.reshape(-1)
    onehot = (flat_e[:, None] == jnp.arange(N_EXPERTS)[None, :]).astype(jnp.int32)
    counts = jnp.sum(onehot, axis=0)
    rank = jnp.sum((jnp.cumsum(onehot, axis=0) - onehot) * onehot, axis=1)
    padded = ((counts + tm - 1) // tm) * tm
    ends = jnp.cumsum(padded)
    starts = ends - padded
    pos = starts[flat_e] + rank
    n_tiles = (TOP_K * nt) // tm + N_EXPERTS
    slots = n_tiles * tm
    token_of_slot = jnp.zeros((slots,), jnp.int32).at[pos].set(jnp.arange(TOP_K * nt, dtype=jnp.int32) // TOP_K)
    tile_start = jnp.arange(n_tiles, dtype=jnp.int32) * tm
    tile_expert = jnp.sum((tile_start[:, None] >= ends[None, :]).astype(jnp.int32), axis=1)
    tile_valid = (tile_start < ends[-1]).astype(jnp.int32)
    last_expert = jnp.max(jnp.where(counts > 0, jnp.arange(N_EXPERTS), 0))
    tile_expert = jnp.where(tile_valid > 0, tile_expert, last_expert).astype(jnp.int32)
    xs = jnp.take(h, token_of_slot, axis=0)
    ys = _moe_ffn(xs, tile_expert, tile_valid, w_gate_up, w_down)
    pos2 = pos.reshape(nt, TOP_K)
    ya = jnp.take(ys, pos2[:, 0], axis=0)
    yb = jnp.take(ys, pos2[:, 1], axis=0)
    return _moe_combine(x, ya, yb, route, mod, bc, n, b)


def _rope_tables(bc, n):
    rows = n // GRID_W
    row = jnp.broadcast_to(jnp.arange(rows)[:, None], (rows, GRID_W)).reshape(-1).astype(F32)
    col = jnp.broadcast_to(jnp.arange(GRID_W)[None, :], (rows, GRID_W)).reshape(-1).astype(F32)
    axis_dim = HEAD_DIM // 2
    inv = ROPE_THETA ** (-jnp.arange(0, axis_dim, 2, dtype=F32) / axis_dim)
    ar = row[:, None] * inv
    ac = col[:, None] * inv
    cos = jnp.concatenate([jnp.cos(ar), jnp.cos(ar), jnp.cos(ac), jnp.cos(ac)], axis=1)
    sin = jnp.concatenate([-jnp.sin(ar), jnp.sin(ar), -jnp.sin(ac), jnp.sin(ac)], axis=1)
    cos = jnp.tile(cos, (1, LANES // HEAD_DIM))
    sin = jnp.tile(sin, (1, LANES // HEAD_DIM))
    return cos, sin


def kernel(x, c, ctx, c_ctx, ada_w, ada_b, norm_mix_g, norm_ffn_g, ssm_a_re, ssm_a_im, ssm_log_dt,
           ssm_b_re, ssm_b_im, ssm_c_re, ssm_c_im, ssm_d, ssm_glu_w, ssm_glu_b, attn_w_qkv, attn_q_g,
           attn_k_g, attn_w_o, ffn_w_gate_up, ffn_w_down, moe_router_w, moe_router_b, moe_w_gate_up,
           moe_w_down):
    b, n, _ = x.shape
    c_len = ctx.shape[1]
    bc = b * c_len
    assert b + 1 <= MOD_ROWS and bc % ROW_TILE == 0 and n % ROW_TILE == 0
    assert c_len % SSM_CHUNK == 0 and n % SSM_CHUNK == 0 and n % GRID_W == 0

    cvecs = jnp.zeros((MOD_ROWS, D_MODEL), F32).at[:b].set(c).at[b].set(c_ctx)
    mods = _modulation_table(cvecs, ada_w, ada_b).reshape(DEPTH, MOD_ROWS, 1, N_MOD * D_MODEL)

    xs = jnp.concatenate([ctx.reshape(bc, D_MODEL), x.reshape(b * n, D_MODEL)], axis=0)

    cos_l, sin_l = _rope_tables(bc, n)
    cos = jnp.concatenate([jnp.ones((bc, LANES), F32), jnp.tile(cos_l, (b, 1))], axis=0)
    sin = jnp.concatenate([jnp.zeros((bc, LANES), F32), jnp.tile(sin_l, (b, 1))], axis=0)
    dq = N_Q_HEADS * HEAD_DIM
    hid = jnp.arange(dq) // HEAD_DIM
    ones = ((hid[:, None] == hid[None, :]).astype(F32) / HEAD_DIM).astype(BF16)

    for i in range(DEPTH):
        j = i // 2
        last = i == DEPTH - 1
        mod = mods[i]
        g_mix = norm_mix_g[i].reshape(1, D_MODEL)
        g_ffn = norm_ffn_g[i].reshape(1, D_MODEL)
        if i % 2 == 0:
            ssm = (ssm_a_re[j], ssm_a_im[j], ssm_log_dt[j], ssm_b_re[j], ssm_b_im[j], ssm_c_re[j],
                   ssm_c_im[j], ssm_d[j], ssm_glu_w[j], ssm_glu_b[j])
            xs = _s5_layer(xs, g_mix, mod, ssm, bc, n, b)
            xs = _ffn(xs, g_ffn, mod, ffn_w_gate_up[j].astype(BF16), ffn_w_down[j].astype(BF16), bc, n, b)
        else:
            qz, k, v = _qkv(xs, g_mix, mod, attn_w_qkv[j].astype(BF16),
                            jnp.tile(attn_q_g[j], N_Q_HEADS).reshape(1, dq),
                            jnp.tile(attn_k_g[j], N_KV_HEADS).reshape(1, N_KV_HEADS * HEAD_DIM),
                            ones, cos, sin, bc, n, b)
            o = _attn_latent(qz, k, v, bc, n, b)
            if last:
                xs = xs[bc:]
                lay = (0, n, b)
            else:
                o = jnp.concatenate([_attn_ctx(qz, k, v, bc, b), o], axis=0)
                lay = (bc, n, b)
            rw = jnp.zeros((D_MODEL, LANES), F32).at[:, :N_EXPERTS].set(moe_router_w[j])
            rb = jnp.zeros((1, LANES), F32).at[0, :N_EXPERTS].set(moe_router_b[j])
            xs, h, route = _attn_out(o, xs, attn_w_o[j].astype(BF16), g_ffn, mod, rw, rb, *lay)
            xs = _moe_layer(xs, h, route, mod, moe_w_gate_up[j].astype(BF16), moe_w_down[j].astype(BF16), *lay)
    return xs.reshape(b, n, D_MODEL)
```

```python
import functools
import math

import jax
import jax.numpy as jnp
from jax import lax
from jax.experimental import pallas as pl
from jax.experimental.pallas import tpu as pltpu

F32 = jnp.float32
BF16 = jnp.bfloat16

D_MODEL = 1024
DEPTH = 4
GRID_W = 64
SSM_GROUP = 16
SSM_GROUPS = D_MODEL // SSM_GROUP
SSM_STATE = 64
HEAD_DIM = 64
N_Q_HEADS = D_MODEL // HEAD_DIM
N_KV_HEADS = 4
Q_PER_KV = N_Q_HEADS // N_KV_HEADS
ROPE_THETA = 10000.0
D_FF = 2816
N_EXPERTS = 8
TOP_K = 2
N_MOD = 6
EPS = 1e-6

LANES = 128
MOD_ROWS = 8
BF16_ROWS = 16
CAST_BLOCK_BYTES = 8 * 1024 * 1024

ROW_TILE = 512
FF_TILE = 1408
MOE_TILE = 512
SSM_CHUNK = 16
SSM_PAIR = 2 * SSM_GROUP
N_PAIRS = D_MODEL // SSM_PAIR
SCAN_ROWS = 8
QUAD = LANES // SSM_PAIR
N_QUADS = 2 * SSM_CHUNK - QUAD
PAIR_W = SSM_CHUNK * SSM_PAIR
Q_TILE = 1024
ATTN_SUB = 64
ATTN_DEPTH = 3
ATTN_KEY_CHUNK = 1024
VMEM_LIMIT = 56 * 1024 * 1024


def _cparams(sem):
    return pltpu.CompilerParams(dimension_semantics=sem, vmem_limit_bytes=VMEM_LIMIT)


def _row_class(t, tile, bc, n, b):
    r0 = t * tile
    return jnp.where(r0 < bc, b, (r0 - bc) // max(n, 1))


def _mod_spec(piece, tile, bc, n, b):
    return pl.BlockSpec((None, 1, D_MODEL),
                        lambda t, *_: (_row_class(t, tile, bc, n, b), 0, piece))


def _norm_mod(x, g, shift, scale):
    ms = jnp.mean(x * x, axis=-1, keepdims=True)
    return (x * lax.rsqrt(ms + EPS) * g) * (1.0 + scale) + shift


def _cast_kernel(w_ref, o_ref):
    o_ref[...] = w_ref[...].astype(BF16)


def _to_bf16(w, j):
    cols = w.shape[-1]
    w3 = w.reshape(w.shape[0], -1, cols)
    rows = w3.shape[1]
    rb = 1 << ((CAST_BLOCK_BYTES // (4 * cols)).bit_length() - 1)
    while rows % rb:
        rb //= 2
    assert rb % BF16_ROWS == 0
    out = pl.pallas_call(
        _cast_kernel,
        grid=(rows // rb,),
        in_specs=[pl.BlockSpec((None, rb, cols), lambda i: (j, i, 0))],
        out_specs=pl.BlockSpec((rb, cols), lambda i: (i, 0)),
        out_shape=jax.ShapeDtypeStruct((rows, cols), BF16),
        compiler_params=_cparams(("parallel",)),
        name="weights_to_bf16",
    )(w3)
    return out.reshape(w.shape[1:])


def _mod_kernel(c_ref, w_ref, b_ref, o_ref):
    c = c_ref[...]
    s = (c * jax.nn.sigmoid(c)).astype(BF16)
    o_ref[...] = jnp.dot(s, w_ref[...].astype(BF16), preferred_element_type=F32) + b_ref[...]


def _modulation_table(cvecs, ada_w, ada_b):
    tn = 2048
    width = N_MOD * D_MODEL
    return pl.pallas_call(
        _mod_kernel,
        grid=(DEPTH, width // tn),
        in_specs=[pl.BlockSpec((MOD_ROWS, D_MODEL), lambda i, j: (0, 0)),
                  pl.BlockSpec((None, D_MODEL, tn), lambda i, j: (i, 0, j)),
                  pl.BlockSpec((None, 1, tn), lambda i, j: (i, 0, j))],
        out_specs=pl.BlockSpec((None, MOD_ROWS, tn), lambda i, j: (i, 0, j)),
        out_shape=jax.ShapeDtypeStruct((DEPTH, MOD_ROWS, width), F32),
        compiler_params=_cparams(("parallel", "parallel")),
        name="adaln_table",
    )(cvecs, ada_w, ada_b.reshape(DEPTH, 1, width))


def _prenorm_kernel(x_ref, g_ref, sh_ref, sc_ref, u_ref, h_scr):
    tm = x_ref.shape[0]
    r = tm // SSM_CHUNK
    h = _norm_mod(x_ref[...], g_ref[...], sh_ref[...], sc_ref[...])
    ncol = D_MODEL // LANES
    for c in range(ncol):
        h_scr[c] = h[:, c * LANES:(c + 1) * LANES]
    lane_q = lax.broadcasted_iota(jnp.int32, (r, LANES), 1) // SSM_PAIR
    for c in range(ncol):
        steps = [h_scr[c, pl.ds(k, r, stride=SSM_CHUNK), :] for k in range(SSM_CHUNK)]
        for pp in range(QUAD):
            cols = []
            for kc in range(SSM_CHUNK // QUAD):
                acc = None
                for jq in range(QUAD):
                    v = steps[QUAD * kc + jq]
                    shift = ((jq - pp) * SSM_PAIR) % LANES
                    if shift:
                        v = pltpu.roll(v, shift, 1)
                    acc = v if acc is None else jnp.where(lane_q == jq, v, acc)
                cols.append(acc)
            u_ref[QUAD * c + pp] = jnp.concatenate(cols, axis=1).astype(BF16)


def _prenorm(x, g, mod, bc, n, b):
    nt = x.shape[0]
    tm = ROW_TILE
    r = tm // SSM_CHUNK
    row = pl.BlockSpec((tm, D_MODEL), lambda t: (t, 0))
    return pl.pallas_call(
        _prenorm_kernel,
        grid=(nt // tm,),
        in_specs=[row, pl.BlockSpec((1, D_MODEL), lambda t: (0, 0)),
                  _mod_spec(0, tm, bc, n, b), _mod_spec(1, tm, bc, n, b)],
        out_specs=pl.BlockSpec((N_PAIRS, r, PAIR_W), lambda t: (0, t, 0)),
        out_shape=jax.ShapeDtypeStruct((N_PAIRS, nt // SSM_CHUNK, PAIR_W), BF16),
        scratch_shapes=[pltpu.VMEM((D_MODEL // LANES, tm, LANES), F32)],
        compiler_params=_cparams(("parallel",)),
        name="s5_prenorm",
    )(x, g, mod, mod)


def _cmul_add(xr, xi, lr, li, tr, ti):
    return xr + lr * tr - li * ti, xi + lr * ti + li * tr


def _block_scan(er, ei, tab_r, tab_i, s_r, s_i, row, reverse):
    xr, xi = er, ei
    for k in (1, 2, 4):
        if reverse:
            keep = row < SCAN_ROWS - k
            lr, li = tab_r[SCAN_ROWS - k:SCAN_ROWS - k + 1], tab_i[SCAN_ROWS - k:SCAN_ROWS - k + 1]
            shift = SCAN_ROWS - k
        else:
            keep = row >= k
            lr, li = tab_r[k - 1:k], tab_i[k - 1:k]
            shift = k
        tr = jnp.where(keep, pltpu.roll(xr, shift, 0), 0.0)
        ti = jnp.where(keep, pltpu.roll(xi, shift, 0), 0.0)
        xr, xi = _cmul_add(xr, xi, lr, li, tr, ti)
    ar, ai = _cmul_add(xr, xi, tab_r, tab_i, s_r, s_i)
    if reverse:
        edge = row == SCAN_ROWS - 1
        inr = jnp.where(edge, s_r, pltpu.roll(ar, SCAN_ROWS - 1, 0))
        ini = jnp.where(edge, s_i, pltpu.roll(ai, SCAN_ROWS - 1, 0))
        return inr, ini, ar[0:1], ai[0:1]
    edge = row == 0
    inr = jnp.where(edge, s_r, pltpu.roll(ar, 1, 0))
    ini = jnp.where(edge, s_i, pltpu.roll(ai, 1, 0))
    return inr, ini, ar[SCAN_ROWS - 1:SCAN_ROWS], ai[SCAN_ROWS - 1:SCAN_ROWS]


def _s5_core_kernel(u_ref, w3_ref, w2t_ref, quad_ref, tab_ref, y_ref, e_scr, s_scr, tz_scr, *,
                    b, n_ctx_chunks, n_chunks):
    for k in range(SSM_CHUNK):
        for q in range(SSM_CHUNK // QUAD):
            tz_scr[k * SSM_PAIR:(k + 1) * SSM_PAIR, q * LANES:(q + 1) * LANES] = (
                quad_ref[QUAD * q - k + SSM_CHUNK - 1])
    u = u_ref[...]
    e_scr[...] = jnp.dot(u, w3_ref[...], preferred_element_type=F32)
    q4 = LANES
    tfr, tfi, trr, tri = tab_ref[0], tab_ref[1], tab_ref[2], tab_ref[3]
    row = lax.broadcasted_iota(jnp.int32, (SCAN_ROWS, q4), 0)
    nb_ctx = n_ctx_chunks // SCAN_ROWS
    nb = n_chunks // SCAN_ROWS

    def body(m, carry):
        mr = jnp.where(m < nb_ctx, nb_ctx - 1 - m, nb - 1 - (m - nb_ctx))
        n_lat_chunks = n_chunks - n_ctx_chunks

        def block_rows(bi, blk):
            start = jnp.where(blk < nb_ctx, bi * n_ctx_chunks + blk * SCAN_ROWS,
                              b * n_ctx_chunks + bi * n_lat_chunks + (blk - nb_ctx) * SCAN_ROWS)
            return pl.ds(pl.multiple_of(start, SCAN_ROWS), SCAN_ROWS)

        out = []
        for bi in range(b):
            sfr, sfi, srr, sri = carry[4 * bi:4 * bi + 4]
            rf = block_rows(bi, m)
            rr = block_rows(bi, mr)
            inr, ini, sfr, sfi = _block_scan(e_scr[rf, 0:q4], e_scr[rf, q4:2 * q4], tfr, tfi, sfr, sfi, row, False)
            s_scr[rf, 0:q4] = inr
            s_scr[rf, q4:2 * q4] = ini
            inr, ini, srr, sri = _block_scan(e_scr[rr, 2 * q4:3 * q4], e_scr[rr, 3 * q4:4 * q4], trr, tri,
                                             srr, sri, row, True)
            s_scr[rr, 2 * q4:3 * q4] = inr
            s_scr[rr, 3 * q4:4 * q4] = ini
            out += [sfr, sfi, srr, sri]
        return tuple(out)

    z = jnp.zeros((1, q4), F32)
    lax.fori_loop(0, nb, body, (z,) * (4 * b))
    y = jnp.dot(u, tz_scr[...], preferred_element_type=F32)
    y = y + lax.dot_general(s_scr[...].astype(BF16), w2t_ref[...], (((1,), (1,)), ((), ())),
                            preferred_element_type=F32)
    y_ref[...] = y.astype(BF16)


def _s5_core(u, w3, w2t, quads, tab, b, n_ctx_chunks, n_chunks):
    rows = u.shape[1]
    mat = pl.BlockSpec((None, PAIR_W, PAIR_W), lambda p: (p, 0, 0))
    seq = pl.BlockSpec((None, rows, PAIR_W), lambda p: (p, 0, 0))
    return pl.pallas_call(
        functools.partial(_s5_core_kernel, b=b, n_ctx_chunks=n_ctx_chunks, n_chunks=n_chunks),
        grid=(N_PAIRS,),
        in_specs=[seq, mat, mat,
                  pl.BlockSpec((None, N_QUADS, SSM_PAIR, LANES), lambda p: (p, 0, 0, 0)),
                  pl.BlockSpec((None, 4, SCAN_ROWS, LANES), lambda p: (p, 0, 0, 0))],
        out_specs=seq,
        out_shape=jax.ShapeDtypeStruct((N_PAIRS, rows, PAIR_W), BF16),
        scratch_shapes=[pltpu.VMEM((rows, PAIR_W), F32), pltpu.VMEM((rows, PAIR_W), F32),
                        pltpu.VMEM((PAIR_W, PAIR_W), BF16)],
        compiler_params=_cparams(("parallel",)),
        name="s5_core",
    )(u, w3, w2t, quads, tab)


def _s5_out_kernel(x_ref, yp_ref, g_ref, sh_ref, sc_ref, gt_ref, d_ref, w_ref, wb_ref, o_ref, y_scr):
    x = x_ref[...]
    tm = x.shape[0]
    r = tm // SSM_CHUNK
    lane_q = lax.broadcasted_iota(jnp.int32, (r, LANES), 1) // SSM_PAIR
    ncol = D_MODEL // LANES
    for c in range(ncol):
        for kc in range(SSM_CHUNK // QUAD):
            src = [yp_ref[QUAD * c + pp, :, kc * LANES:(kc + 1) * LANES].astype(F32) for pp in range(QUAD)]
            for jq in range(QUAD):
                acc = None
                for pp in range(QUAD):
                    v = src[pp]
                    shift = ((pp - jq) * SSM_PAIR) % LANES
                    if shift:
                        v = pltpu.roll(v, shift, 1)
                    acc = v if acc is None else jnp.where(lane_q == pp, v, acc)
                y_scr[c, pl.ds(QUAD * kc + jq, r, stride=SSM_CHUNK), :] = acc
    y = jnp.concatenate([y_scr[c] for c in range(ncol)], axis=1)
    h = _norm_mod(x, g_ref[...], sh_ref[...], sc_ref[...])
    yy = y + d_ref[...] * h
    z = jax.nn.gelu(yy).astype(BF16)
    ag = jnp.dot(z, w_ref[...], preferred_element_type=F32) + wb_ref[...]
    a = ag[:, :D_MODEL]
    g = ag[:, D_MODEL:]
    o_ref[...] = x + gt_ref[...] * (a * jax.nn.sigmoid(g))


def _s5_out(x, yp, g, mod, d, glu_w, glu_b, bc, n, b):
    nt = x.shape[0]
    tm = ROW_TILE
    row = pl.BlockSpec((tm, D_MODEL), lambda t: (t, 0))
    vec = pl.BlockSpec((1, D_MODEL), lambda t: (0, 0))
    return pl.pallas_call(
        _s5_out_kernel,
        grid=(nt // tm,),
        in_specs=[row, pl.BlockSpec((N_PAIRS, tm // SSM_CHUNK, PAIR_W), lambda t: (0, t, 0)), vec,
                  _mod_spec(0, tm, bc, n, b), _mod_spec(1, tm, bc, n, b), _mod_spec(2, tm, bc, n, b),
                  vec,
                  pl.BlockSpec((D_MODEL, 2 * D_MODEL), lambda t: (0, 0)),
                  pl.BlockSpec((1, 2 * D_MODEL), lambda t: (0, 0))],
        out_specs=row,
        out_shape=jax.ShapeDtypeStruct((nt, D_MODEL), F32),
        scratch_shapes=[pltpu.VMEM((D_MODEL // LANES, tm, LANES), F32)],
        compiler_params=_cparams(("parallel",)),
        name="s5_out",
    )(x, yp, g, mod, mod, mod, d, glu_w, glu_b)


def _pair_lanes(x):
    z = jnp.zeros_like(x[..., 0, :, :])
    top = jnp.concatenate([x[..., 0, :, :], z], axis=-1)
    bot = jnp.concatenate([z, x[..., 1, :, :]], axis=-1)
    return jnp.stack([top, bot], axis=-3)


def _s5_chunk_weights(a_re, a_im, log_dt, b_re, b_im, c_re, c_im):
    hp = lax.Precision.HIGHEST
    L = SSM_CHUNK
    G, P, I = SSM_GROUPS, SSM_STATE, SSM_GROUP
    a_re = a_re.astype(F32)
    a_im = a_im.astype(F32)
    dt = jnp.exp(log_dt.astype(F32))[..., None]
    mag = jnp.exp(a_re * dt)
    lam_re = mag * jnp.cos(a_im * dt)
    lam_im = mag * jnp.sin(a_im * dt)
    den = a_re * a_re + a_im * a_im
    num_re = lam_re - 1.0
    f_re = (num_re * a_re + lam_im * a_im) / den
    f_im = (lam_im * a_re - num_re * a_im) / den
    b_re = b_re.astype(F32)
    b_im = b_im.astype(F32)
    bb_re = f_re[..., None] * b_re - f_im[..., None] * b_im
    bb_im = f_re[..., None] * b_im + f_im[..., None] * b_re
    m = jnp.arange(L + 1, dtype=F32)[:, None, None, None]
    pmag = jnp.exp(m * (a_re * dt)[None])
    pw_re = pmag * jnp.cos(m * (a_im * dt)[None])
    pw_im = pmag * jnp.sin(m * (a_im * dt)[None])
    c_re = c_re.astype(F32)
    c_im = c_im.astype(F32)
    cl_re = c_re[None] * pw_re[:, :, :, None, :] - c_im[None] * pw_im[:, :, :, None, :]
    cl_im = c_re[None] * pw_im[:, :, :, None, :] + c_im[None] * pw_re[:, :, :, None, :]
    bt_re = jnp.swapaxes(bb_re, -1, -2)
    bt_im = jnp.swapaxes(bb_im, -1, -2)
    lb_re = pw_re[:, :, :, None, :] * bt_re[None] - pw_im[:, :, :, None, :] * bt_im[None]
    lb_im = pw_re[:, :, :, None, :] * bt_im[None] + pw_im[:, :, :, None, :] * bt_re[None]

    def pair_rows(parts):
        cols = []
        for x in parts:
            x = x.reshape(L, N_PAIRS, 2, x.shape[-2], P)
            cols.append(_pair_lanes(jnp.transpose(x, (1, 0, 2, 3, 4))))
        return jnp.concatenate(cols, axis=-1).reshape(N_PAIRS, PAIR_W, 4 * 2 * P)

    pf = (L - 1) - jnp.arange(L)
    pr = jnp.arange(L)
    w3 = pair_rows([lb_re[pf, 0], lb_im[pf, 0], lb_re[pr, 1], lb_im[pr, 1]])
    qf = jnp.arange(L) + 1
    qr = L - jnp.arange(L)
    w2t = pair_rows([cl_re[qf, 0], -cl_im[qf, 0], cl_re[qr, 1], -cl_im[qr, 1]])
    kk = (jnp.einsum('mrgop,rgpi->mrgio', cl_re[:L], bb_re, precision=hp)
          - jnp.einsum('mrgop,rgpi->mrgio', cl_im[:L], bb_im, precision=hp))
    kp = _pair_lanes(kk.reshape(L, 2, N_PAIRS, 2, I, I)).reshape(L, 2, N_PAIRS, SSM_PAIR, SSM_PAIR)
    lags = jnp.concatenate([kp[:0:-1, 1], kp[:1, 0] + kp[:1, 1], kp[1:, 0]], axis=0)
    quads = jnp.concatenate([lags[q:q + N_QUADS] for q in range(QUAD)], axis=-1)
    quads = jnp.transpose(quads, (1, 0, 2, 3))
    jf = (jnp.arange(SCAN_ROWS, dtype=F32) + 1.0) * L
    jr = (SCAN_ROWS - jnp.arange(SCAN_ROWS, dtype=F32)) * L
    jj = jnp.stack([jf, jr], axis=0)[:, :, None, None]
    tmag = jnp.exp(jj * (a_re * dt)[:, None])
    t_re = tmag * jnp.cos(jj * (a_im * dt)[:, None])
    t_im = tmag * jnp.sin(jj * (a_im * dt)[:, None])
    tab = jnp.stack([t_re[0], t_im[0], t_re[1], t_im[1]], axis=0)
    tab = jnp.transpose(tab.reshape(4, SCAN_ROWS, N_PAIRS, 2 * P), (2, 0, 1, 3))
    return w3.astype(BF16), w2t.astype(BF16), quads.astype(BF16), tab


def _s5_layer(x, g, mod, ssm, bc, n, b):
    (a_re, a_im, log_dt, b_re, b_im, c_re, c_im, d, glu_w, glu_b) = ssm
    c_len = bc // b
    n_ctx_chunks = c_len // SSM_CHUNK
    n_chunks = (c_len + n) // SSM_CHUNK
    u = _prenorm(x, g, mod, bc, n, b)
    w3, w2t, quads, tab = _s5_chunk_weights(a_re, a_im, log_dt, b_re, b_im, c_re, c_im)
    yp = _s5_core(u, w3, w2t, quads, tab, b, n_ctx_chunks, n_chunks)
    return _s5_out(x, yp, g, mod, d.reshape(1, D_MODEL), glu_w.astype(BF16),
                   glu_b.reshape(1, 2 * D_MODEL), bc, n, b)


def _ffn_kernel(x_ref, g_ref, sh_ref, sc_ref, gt_ref, wg_ref, wu_ref, wd_ref, o_ref, h_scr, acc_scr):
    j = pl.program_id(1)

    @pl.when(j == 0)
    def _():
        h_scr[...] = _norm_mod(x_ref[...], g_ref[...], sh_ref[...], sc_ref[...]).astype(BF16)
        acc_scr[...] = jnp.zeros_like(acc_scr)

    h = h_scr[...]
    gate = jnp.dot(h, wg_ref[...], preferred_element_type=F32)
    up = jnp.dot(h, wu_ref[...], preferred_element_type=F32)
    a = (gate * jax.nn.sigmoid(gate) * up).astype(BF16)
    acc_scr[...] += jnp.dot(a, wd_ref[...], preferred_element_type=F32)

    @pl.when(j == pl.num_programs(1) - 1)
    def _():
        o_ref[...] = x_ref[...] + gt_ref[...] * acc_scr[...]


def _ffn(x, g, mod, w_gate_up, w_down, bc, n, b):
    nt = x.shape[0]
    tm = ROW_TILE
    nj = D_FF // FF_TILE
    row = pl.BlockSpec((tm, D_MODEL), lambda t, j: (t, 0))
    return pl.pallas_call(
        _ffn_kernel,
        grid=(nt // tm, nj),
        in_specs=[row, pl.BlockSpec((1, D_MODEL), lambda t, j: (0, 0)),
                  _mod_spec(3, tm, bc, n, b), _mod_spec(4, tm, bc, n, b), _mod_spec(5, tm, bc, n, b),
                  pl.BlockSpec((D_MODEL, FF_TILE), lambda t, j: (0, j)),
                  pl.BlockSpec((D_MODEL, FF_TILE), lambda t, j: (0, nj + j)),
                  pl.BlockSpec((FF_TILE, D_MODEL), lambda t, j: (j, 0))],
        out_specs=row,
        out_shape=jax.ShapeDtypeStruct((nt, D_MODEL), F32),
        scratch_shapes=[pltpu.VMEM((tm, D_MODEL), BF16), pltpu.VMEM((tm, D_MODEL), F32)],
        compiler_params=_cparams(("parallel", "arbitrary")),
        name="ffn_dense",
    )(x, g, mod, mod, mod, w_gate_up, w_gate_up, w_down)


def _head_ms(sq, ones):
    return jnp.dot(sq.astype(BF16), ones, preferred_element_type=F32)


def _rope(v, cos, sin_signed, first_half):
    w = v.shape[-1]
    quarter = HEAD_DIM // 4
    swapped = jnp.where(first_half, pltpu.roll(v, w - quarter, 1), pltpu.roll(v, quarter, 1))
    return v * cos + swapped * sin_signed


def _qkv_kernel(x_ref, g_ref, sh_ref, sc_ref, w_ref, qg_ref, kg_ref, ones_ref, cos_ref, sin_ref,
                q_ref, k_ref, v_ref):
    tm = x_ref.shape[0]
    h = _norm_mod(x_ref[...], g_ref[...], sh_ref[...], sc_ref[...]).astype(BF16)
    qkv = jnp.dot(h, w_ref[...], preferred_element_type=F32)
    dq = N_Q_HEADS * HEAD_DIM
    dk = N_KV_HEADS * HEAD_DIM
    q = qkv[:, :dq]
    k = qkv[:, dq:dq + dk]
    v = qkv[:, dq + dk:]
    ones = ones_ref[...]
    q = q * lax.rsqrt(_head_ms(q * q, ones) + EPS) * qg_ref[...]
    k = k * lax.rsqrt(_head_ms(k * k, ones[:dk, :dk]) + EPS) * kg_ref[...]
    cos = cos_ref[...]
    sin = sin_ref[...]
    lane_q = lax.broadcasted_iota(jnp.int32, (tm, dq), 1)
    lane_k = lax.broadcasted_iota(jnp.int32, (tm, dk), 1)
    half = HEAD_DIM // 2
    quarter = HEAD_DIM // 4
    q = _rope(q, jnp.tile(cos, (1, dq // LANES)), jnp.tile(sin, (1, dq // LANES)), (lane_q % half) < quarter)
    k = _rope(k, jnp.tile(cos, (1, dk // LANES)), jnp.tile(sin, (1, dk // LANES)), (lane_k % half) < quarter)
    q = q * (HEAD_DIM ** -0.5)
    k_ref[...] = k.astype(BF16)
    v_ref[...] = v.astype(BF16)
    lane = lax.broadcasted_iota(jnp.int32, (tm, LANES), 1)
    low = lane < HEAD_DIM
    for c in range(dq // LANES):
        vc = q[:, c * LANES:(c + 1) * LANES]
        vr = pltpu.roll(vc, HEAD_DIM, 1)
        kv_half = (c // 2) % 2
        if kv_half == 0:
            first = jnp.where(low, vc, 0.0)
            second = jnp.where(low, vr, 0.0)
        else:
            first = jnp.where(low, 0.0, vr)
            second = jnp.where(low, 0.0, vc)
        q_ref[:, (2 * c) * LANES:(2 * c + 1) * LANES] = first.astype(BF16)
        q_ref[:, (2 * c + 1) * LANES:(2 * c + 2) * LANES] = second.astype(BF16)


def _qkv(x, g, mod, w_qkv, q_g, k_g, ones, cos, sin, bc, n, b):
    nt = x.shape[0]
    tm = ROW_TILE
    dq = N_Q_HEADS * HEAD_DIM
    dk = N_KV_HEADS * HEAD_DIM
    row = lambda w: pl.BlockSpec((tm, w), lambda t: (t, 0))
    full = lambda r, w: pl.BlockSpec((r, w), lambda t: (0, 0))
    return pl.pallas_call(
        _qkv_kernel,
        grid=(nt // tm,),
        in_specs=[row(D_MODEL), full(1, D_MODEL), _mod_spec(0, tm, bc, n, b), _mod_spec(1, tm, bc, n, b),
                  full(D_MODEL, dq + 2 * dk), full(1, dq), full(1, dk), full(dq, dq),
                  row(LANES), row(LANES)],
        out_specs=[row(2 * dq), row(dk), row(dk)],
        out_shape=[jax.ShapeDtypeStruct((nt, 2 * dq), BF16),
                   jax.ShapeDtypeStruct((nt, dk), BF16),
                   jax.ShapeDtypeStruct((nt, dk), BF16)],
        compiler_params=_cparams(("parallel",)),
        name="attn_qkv",
    )(x, g, mod, mod, w_qkv, q_g, k_g, ones, cos, sin)


def _attn_kernel(*refs, sub, n_sub, n_lat):
    if n_lat:
        q_ref, kc_ref, vc_ref, kl_ref, vl_ref, o_ref = refs[:6]
    else:
        q_ref, kc_ref, vc_ref, o_ref = refs[:4]
    s_bufs = refs[-2 * ATTN_DEPTH:-ATTN_DEPTH]
    m_bufs = refs[-ATTN_DEPTH:]
    grp = pl.program_id(1)
    nt_dims = (((1,), (1,)), ((), ()))
    c_len = kc_ref.shape[0]
    parts = [(kc_ref, vc_ref, 0, 0, c_len)]
    if n_lat:
        kc = min(ATTN_KEY_CHUNK, n_lat)
        parts += [(kl_ref, vl_ref, c0, c_len + c0, kc) for c0 in range(0, n_lat, kc)]
    lane = lax.broadcasted_iota(jnp.int32, (sub, LANES), 1)
    low = lane < HEAD_DIM
    upper_half = (grp % 2) == 1

    def scores(i, s_scr, m_scr):
        r0 = pl.multiple_of(i * sub, sub)
        qs = jnp.concatenate([q_ref[pl.ds(r0, sub), a * LANES:(a + 1) * LANES] for a in range(Q_PER_KV)],
                             axis=0)
        m = None
        for kr, _, k0, off, sz in parts:
            s = lax.dot_general(qs, kr[k0:k0 + sz, :], nt_dims, preferred_element_type=F32)
            s_scr[:, off:off + sz] = s
            mx = s.max(axis=-1, keepdims=True)
            m = mx if m is None else jnp.maximum(m, mx)
        m_scr[...] = m

    def output(i, s_scr, m_scr):
        r0 = pl.multiple_of(i * sub, sub)
        m = m_scr[...]
        l = jnp.zeros_like(m)
        acc = jnp.zeros((Q_PER_KV * sub, LANES), F32)
        for _, vr, k0, off, sz in parts:
            p = jnp.exp(s_scr[:, off:off + sz] - m)
            l = l + p.sum(axis=-1, keepdims=True)
            acc = acc + jnp.dot(p.astype(BF16), vr[k0:k0 + sz, :], preferred_element_type=F32)
        o = acc / l
        for c in range(Q_PER_KV // 2):
            a0 = o[(2 * c) * sub:(2 * c + 1) * sub]
            a1 = o[(2 * c + 1) * sub:(2 * c + 2) * sub]
            from_low = jnp.where(low, a0, pltpu.roll(a1, HEAD_DIM, 1))
            from_high = jnp.where(low, pltpu.roll(a0, HEAD_DIM, 1), a1)
            o_ref[pl.ds(r0, sub), c * LANES:(c + 1) * LANES] = (
                jnp.where(upper_half, from_high, from_low).astype(BF16))

    depth = ATTN_DEPTH
    ahead = depth - 1

    def stage(i, r):
        output(i, s_bufs[r], m_bufs[r])
        nxt = (r + ahead) % depth
        scores(i + ahead, s_bufs[nxt], m_bufs[nxt])

    for t in range(min(ahead, n_sub)):
        scores(t, s_bufs[t], m_bufs[t])
    n_loop = max(n_sub - ahead, 0) // depth

    def body(j, carry):
        for r in range(depth):
            stage(j * depth + r, r)
        return carry

    if n_loop:
        lax.fori_loop(0, n_loop, body, 0)
    for i in range(n_loop * depth, n_sub):
        if i + ahead < n_sub:
            stage(i, i % depth)
        else:
            output(i, s_bufs[i % depth], m_bufs[i % depth])


def _attn_call(name, tq, n_lat, c_len, grid, in_specs, out_spec, out_rows, args):
    rows = Q_PER_KV * ATTN_SUB
    assert tq % ATTN_SUB == 0
    return pl.pallas_call(
        functools.partial(_attn_kernel, sub=ATTN_SUB, n_sub=tq // ATTN_SUB, n_lat=n_lat),
        grid=grid,
        in_specs=in_specs,
        out_specs=out_spec,
        out_shape=jax.ShapeDtypeStruct((out_rows, D_MODEL), BF16),
        scratch_shapes=([pltpu.VMEM((rows, c_len + n_lat), F32)] * ATTN_DEPTH
                        + [pltpu.VMEM((rows, 1), F32)] * ATTN_DEPTH),
        compiler_params=_cparams(("parallel", "parallel", "parallel")),
        name=name,
    )(*args)


def _attn_latent(qz, k, v, bc, n, b):
    c_len = bc // b
    tq = min(Q_TILE, n)
    assert bc % tq == 0 and n % tq == 0
    gw = Q_PER_KV * LANES
    k_lat, v_lat = k[bc:], v[bc:]
    q_spec = pl.BlockSpec((tq, gw), lambda bi, g, qi: ((bc + bi * n) // tq + qi, g))
    ctx_spec = pl.BlockSpec((c_len, LANES), lambda bi, g, qi: (bi, g // 2))
    lat_spec = pl.BlockSpec((n, LANES), lambda bi, g, qi: (bi, g // 2))
    o_spec = pl.BlockSpec((tq, Q_PER_KV * HEAD_DIM), lambda bi, g, qi: (bi * (n // tq) + qi, g))
    return _attn_call("attn_latent", tq, n, c_len, (b, N_KV_HEADS, n // tq),
                      [q_spec, ctx_spec, ctx_spec, lat_spec, lat_spec], o_spec, b * n,
                      (qz, k, v, k_lat, v_lat))


def _attn_ctx(qz, k, v, bc, b):
    c_len = bc // b
    gw = Q_PER_KV * LANES
    q_spec = pl.BlockSpec((c_len, gw), lambda bi, g, qi: (bi, g))
    ctx_spec = pl.BlockSpec((c_len, LANES), lambda bi, g, qi: (bi, g // 2))
    o_spec = pl.BlockSpec((c_len, Q_PER_KV * HEAD_DIM), lambda bi, g, qi: (bi, g))
    return _attn_call("attn_ctx", c_len, 0, c_len, (b, N_KV_HEADS, 1),
                      [q_spec, ctx_spec, ctx_spec], o_spec, bc, (qz, k, v))


def _attn_out_kernel(o_ref, x_ref, wo_ref, gt_ref, g_ref, sh_ref, sc_ref, rw_ref, rb_ref,
                     xo_ref, h_ref, r_ref):
    tm = x_ref.shape[0]
    x = x_ref[...] + gt_ref[...] * jnp.dot(o_ref[...], wo_ref[...], preferred_element_type=F32)
    xo_ref[...] = x
    h = _norm_mod(x, g_ref[...], sh_ref[...], sc_ref[...])
    h_ref[...] = h
    logits = jnp.dot(h, rw_ref[...], preferred_element_type=F32, precision=lax.Precision.HIGHEST) + rb_ref[...]
    lane = lax.broadcasted_iota(jnp.int32, (tm, LANES), 1)
    neg = -jnp.inf
    logits = jnp.where(lane < N_EXPERTS, logits, neg)
    m1 = jnp.max(logits, axis=-1, keepdims=True)
    i1 = jnp.min(jnp.where(logits == m1, lane, LANES), axis=-1, keepdims=True)
    rest = jnp.where(lane == i1, neg, logits)
    m2 = jnp.max(rest, axis=-1, keepdims=True)
    i2 = jnp.min(jnp.where(rest == m2, lane, LANES), axis=-1, keepdims=True)
    e = jnp.exp(m2 - m1)
    w1 = 1.0 / (1.0 + e)
    w2 = e / (1.0 + e)
    r = jnp.where(lane == 0, i1.astype(F32), 0.0)
    r = jnp.where(lane == 1, i2.astype(F32), r)
    r = jnp.where(lane == 2, w1, r)
    r = jnp.where(lane == 3, w2, r)
    r_ref[...] = r


def _attn_out(o, x, w_o, g, mod, router_w, router_b, bc, n, b):
    nt = x.shape[0]
    tm = ROW_TILE
    row = lambda w: pl.BlockSpec((tm, w), lambda t: (t, 0))
    full = lambda r, w: pl.BlockSpec((r, w), lambda t: (0, 0))
    return pl.pallas_call(
        _attn_out_kernel,
        grid=(nt // tm,),
        in_specs=[row(D_MODEL), row(D_MODEL), full(D_MODEL, D_MODEL), _mod_spec(2, tm, bc, n, b),
                  full(1, D_MODEL), _mod_spec(3, tm, bc, n, b), _mod_spec(4, tm, bc, n, b),
                  full(D_MODEL, LANES), full(1, LANES)],
        out_specs=[row(D_MODEL), row(D_MODEL), row(LANES)],
        out_shape=[jax.ShapeDtypeStruct((nt, D_MODEL), F32),
                   jax.ShapeDtypeStruct((nt, D_MODEL), F32),
                   jax.ShapeDtypeStruct((nt, LANES), F32)],
        compiler_params=_cparams(("parallel",)),
        name="attn_out_router",
    )(o, x, w_o, mod, g, mod, mod, router_w, router_b)


def _moe_ffn_kernel(te_ref, tv_ref, x_ref, wg_ref, wu_ref, wd_ref, o_ref, h_scr, acc_scr):
    i = pl.program_id(0)
    j = pl.program_id(1)
    valid = tv_ref[i] > 0

    @pl.when(j == 0)
    def _():
        h_scr[...] = x_ref[...].astype(BF16)
        acc_scr[...] = jnp.zeros_like(acc_scr)

    @pl.when(valid)
    def _():
        h = h_scr[...]
        gate = jnp.dot(h, wg_ref[...], preferred_element_type=F32)
        up = jnp.dot(h, wu_ref[...], preferred_element_type=F32)
        a = (gate * jax.nn.sigmoid(gate) * up).astype(BF16)
        acc_scr[...] += jnp.dot(a, wd_ref[...], preferred_element_type=F32)

    @pl.when(j == pl.num_programs(1) - 1)
    def _():
        o_ref[...] = acc_scr[...].astype(BF16)


def _moe_ffn(xs, tile_expert, tile_valid, w_gate_up, w_down):
    slots = xs.shape[0]
    tm = MOE_TILE
    nj = D_FF // FF_TILE
    row = pl.BlockSpec((tm, D_MODEL), lambda i, j, te, tv: (i, 0))
    grid_spec = pltpu.PrefetchScalarGridSpec(
        num_scalar_prefetch=2,
        grid=(slots // tm, nj),
        in_specs=[row,
                  pl.BlockSpec((None, D_MODEL, FF_TILE), lambda i, j, te, tv: (te[i], 0, j)),
                  pl.BlockSpec((None, D_MODEL, FF_TILE), lambda i, j, te, tv: (te[i], 0, nj + j)),
                  pl.BlockSpec((None, FF_TILE, D_MODEL), lambda i, j, te, tv: (te[i], j, 0))],
        out_specs=row,
        scratch_shapes=[pltpu.VMEM((tm, D_MODEL), BF16), pltpu.VMEM((tm, D_MODEL), F32)])
    return pl.pallas_call(
        _moe_ffn_kernel,
        grid_spec=grid_spec,
        out_shape=jax.ShapeDtypeStruct((slots, D_MODEL), BF16),
        compiler_params=_cparams(("parallel", "arbitrary")),
        name="moe_ffn",
    )(tile_expert, tile_valid, xs, w_gate_up, w_gate_up, w_down)


def _moe_combine_kernel(x_ref, a_ref, b_ref, r_ref, gt_ref, o_ref):
    tm = x_ref.shape[0]
    lane = lax.broadcasted_iota(jnp.int32, (tm, LANES), 1)
    r = r_ref[...]
    w1 = jnp.sum(jnp.where(lane == 2, r, 0.0), axis=-1, keepdims=True)
    w2 = jnp.sum(jnp.where(lane == 3, r, 0.0), axis=-1, keepdims=True)
    y = w1 * a_ref[...].astype(F32) + w2 * b_ref[...].astype(F32)
    o_ref[...] = x_ref[...] + gt_ref[...] * y


def _moe_combine(x, ya, yb, route, mod, bc, n, b):
    nt = x.shape[0]
    tm = ROW_TILE
    row = lambda w: pl.BlockSpec((tm, w), lambda t: (t, 0))
    return pl.pallas_call(
        _moe_combine_kernel,
        grid=(nt // tm,),
        in_specs=[row(D_MODEL), row(D_MODEL), row(D_MODEL), row(LANES), _mod_spec(5, tm, bc, n, b)],
        out_specs=row(D_MODEL),
        out_shape=jax.ShapeDtypeStruct((nt, D_MODEL), F32),
        compiler_params=_cparams(("parallel",)),
        name="moe_combine",
    )(x, ya, yb, route, mod)


def _moe_layer(x, h, route, mod, w_gate_up, w_down, bc, n, b):
    nt = x.shape[0]
    tm = MOE_TILE
    top_i = route[:, :TOP_K].astype(jnp.int32)
    flat_e = top_i.reshape(-1)
    onehot = (flat_e[:, None] == jnp.arange(N_EXPERTS)[None, :]).astype(jnp.int32)
    counts = jnp.sum(onehot, axis=0)
    rank = jnp.sum((jnp.cumsum(onehot, axis=0) - onehot) * onehot, axis=1)
    padded = ((counts + tm - 1) // tm) * tm
    ends = jnp.cumsum(padded)
    starts = ends - padded
    pos = starts[flat_e] + rank
    n_tiles = (TOP_K * nt) // tm + N_EXPERTS
    slots = n_tiles * tm
    token_of_slot = jnp.zeros((slots,), jnp.int32).at[pos].set(jnp.arange(TOP_K * nt, dtype=jnp.int32) // TOP_K)
    tile_start = jnp.arange(n_tiles, dtype=jnp.int32) * tm
    tile_expert = jnp.sum((tile_start[:, None] >= ends[None, :]).astype(jnp.int32), axis=1)
    tile_valid = (tile_start < ends[-1]).astype(jnp.int32)
    last_expert = jnp.max(jnp.where(counts > 0, jnp.arange(N_EXPERTS), 0))
    tile_expert = jnp.where(tile_valid > 0, tile_expert, last_expert).astype(jnp.int32)
    xs = h.at[token_of_slot].get(mode="promise_in_bounds")
    ys = _moe_ffn(xs, tile_expert, tile_valid, w_gate_up, w_down)
    pos2 = pos.reshape(nt, TOP_K)
    ya = ys.at[pos2[:, 0]].get(mode="promise_in_bounds")
    yb = ys.at[pos2[:, 1]].get(mode="promise_in_bounds")
    return _moe_combine(x, ya, yb, route, mod, bc, n, b)


def _rope_tables(bc, n):
    rows = n // GRID_W
    row = jnp.broadcast_to(jnp.arange(rows)[:, None], (rows, GRID_W)).reshape(-1).astype(F32)
    col = jnp.broadcast_to(jnp.arange(GRID_W)[None, :], (rows, GRID_W)).reshape(-1).astype(F32)
    axis_dim = HEAD_DIM // 2
    inv = ROPE_THETA ** (-jnp.arange(0, axis_dim, 2, dtype=F32) / axis_dim)
    ar = row[:, None] * inv
    ac = col[:, None] * inv
    cos = jnp.concatenate([jnp.cos(ar), jnp.cos(ar), jnp.cos(ac), jnp.cos(ac)], axis=1)
    sin = jnp.concatenate([-jnp.sin(ar), jnp.sin(ar), -jnp.sin(ac), jnp.sin(ac)], axis=1)
    cos = jnp.tile(cos, (1, LANES // HEAD_DIM))
    sin = jnp.tile(sin, (1, LANES // HEAD_DIM))
    return cos, sin


def kernel(x, c, ctx, c_ctx, ada_w, ada_b, norm_mix_g, norm_ffn_g, ssm_a_re, ssm_a_im, ssm_log_dt,
           ssm_b_re, ssm_b_im, ssm_c_re, ssm_c_im, ssm_d, ssm_glu_w, ssm_glu_b, attn_w_qkv, attn_q_g,
           attn_k_g, attn_w_o, ffn_w_gate_up, ffn_w_down, moe_router_w, moe_router_b, moe_w_gate_up,
           moe_w_down):
    b, n, _ = x.shape
    c_len = ctx.shape[1]
    bc = b * c_len
    assert b + 1 <= MOD_ROWS and bc % ROW_TILE == 0 and n % ROW_TILE == 0
    assert c_len % SSM_CHUNK == 0 and n % SSM_CHUNK == 0 and n % GRID_W == 0

    cvecs = jnp.zeros((MOD_ROWS, D_MODEL), F32).at[:b].set(c).at[b].set(c_ctx)
    mods = _modulation_table(cvecs, ada_w, ada_b).reshape(DEPTH, MOD_ROWS, 1, N_MOD * D_MODEL)

    xs = jnp.concatenate([ctx.reshape(bc, D_MODEL), x.reshape(b * n, D_MODEL)], axis=0)

    cos_l, sin_l = _rope_tables(bc, n)
    cos = jnp.concatenate([jnp.ones((bc, LANES), F32), jnp.tile(cos_l, (b, 1))], axis=0)
    sin = jnp.concatenate([jnp.zeros((bc, LANES), F32), jnp.tile(sin_l, (b, 1))], axis=0)
    dq = N_Q_HEADS * HEAD_DIM
    hid = jnp.arange(dq) // HEAD_DIM
    ones = ((hid[:, None] == hid[None, :]).astype(F32) / HEAD_DIM).astype(BF16)

    for i in range(DEPTH):
        j = i // 2
        last = i == DEPTH - 1
        mod = mods[i]
        g_mix = norm_mix_g[i].reshape(1, D_MODEL)
        g_ffn = norm_ffn_g[i].reshape(1, D_MODEL)
        if i % 2 == 0:
            ssm = (ssm_a_re[j], ssm_a_im[j], ssm_log_dt[j], ssm_b_re[j], ssm_b_im[j], ssm_c_re[j],
                   ssm_c_im[j], ssm_d[j], ssm_glu_w[j], ssm_glu_b[j])
            xs = _s5_layer(xs, g_mix, mod, ssm, bc, n, b)
            xs = _ffn(xs, g_ffn, mod, _to_bf16(ffn_w_gate_up, j), _to_bf16(ffn_w_down, j), bc, n, b)
        else:
            qz, k, v = _qkv(xs, g_mix, mod, attn_w_qkv[j].astype(BF16),
                            jnp.tile(attn_q_g[j], N_Q_HEADS).reshape(1, dq),
                            jnp.tile(attn_k_g[j], N_KV_HEADS).reshape(1, N_KV_HEADS * HEAD_DIM),
                            ones, cos, sin, bc, n, b)
            o = _attn_latent(qz, k, v, bc, n, b)
            if last:
                xs = xs[bc:]
                lay = (0, n, b)
            else:
                o = jnp.concatenate([_attn_ctx(qz, k, v, bc, b), o], axis=0)
                lay = (bc, n, b)
            rw = jnp.zeros((D_MODEL, LANES), F32).at[:, :N_EXPERTS].set(moe_router_w[j])
            rb = jnp.zeros((1, LANES), F32).at[0, :N_EXPERTS].set(moe_router_b[j])
            xs, h, route = _attn_out(o, xs, attn_w_o[j].astype(BF16), g_ffn, mod, rw, rb, *lay)
            xs = _moe_layer(xs, h, route, mod, _to_bf16(moe_w_gate_up, j), _to_bf16(moe_w_down, j), *lay)
    return xs.reshape(b, n, D_MODEL)
```

```python
import functools
import math

import jax
import jax.numpy as jnp
from jax import lax
from jax.experimental import pallas as pl
from jax.experimental.pallas import tpu as pltpu

F32 = jnp.float32
BF16 = jnp.bfloat16

D_MODEL = 1024
DEPTH = 4
GRID_W = 64
SSM_GROUP = 16
SSM_GROUPS = D_MODEL // SSM_GROUP
SSM_STATE = 64
HEAD_DIM = 64
N_Q_HEADS = D_MODEL // HEAD_DIM
N_KV_HEADS = 4
Q_PER_KV = N_Q_HEADS // N_KV_HEADS
ROPE_THETA = 10000.0
D_FF = 2816
N_EXPERTS = 8
TOP_K = 2
N_MOD = 6
EPS = 1e-6

LANES = 128
MOD_ROWS = 8
BF16_ROWS = 16
CAST_BLOCK_BYTES = 8 * 1024 * 1024
CAST_STREAMS = 4

ROW_TILE = 512
FF_TILE = 1408
MOE_TILE = 512
SSM_CHUNK = 16
SSM_PAIR = 2 * SSM_GROUP
N_PAIRS = D_MODEL // SSM_PAIR
SCAN_ROWS = 8
QUAD = LANES // SSM_PAIR
N_QUADS = 2 * SSM_CHUNK - QUAD
PAIR_W = SSM_CHUNK * SSM_PAIR
Q_TILE = 1024
ATTN_SUB = 64
ATTN_DEPTH = 3
ATTN_KEY_CHUNK = 1024
VMEM_LIMIT = 56 * 1024 * 1024


def _cparams(sem):
    return pltpu.CompilerParams(dimension_semantics=sem, vmem_limit_bytes=VMEM_LIMIT)


def _row_class(t, tile, bc, n, b):
    r0 = t * tile
    return jnp.where(r0 < bc, b, (r0 - bc) // max(n, 1))


def _mod_spec(piece, tile, bc, n, b):
    return pl.BlockSpec((None, 1, D_MODEL),
                        lambda t, *_: (_row_class(t, tile, bc, n, b), 0, piece))


def _norm_mod(x, g, shift, scale):
    ms = jnp.mean(x * x, axis=-1, keepdims=True)
    return (x * lax.rsqrt(ms + EPS) * g) * (1.0 + scale) + shift


def _cast_kernel(*refs):
    o_ref = refs[-1]
    rq = refs[0].shape[0]
    for q, w_ref in enumerate(refs[:-1]):
        o_ref[q * rq:(q + 1) * rq, :] = w_ref[...].astype(BF16)


def _to_bf16(w, j):
    cols = w.shape[-1]
    w3 = w.reshape(w.shape[0], -1, cols)
    rows = w3.shape[1]
    rb = 1 << ((CAST_BLOCK_BYTES // (4 * cols)).bit_length() - 1)
    while rows % rb:
        rb //= 2
    rq = rb // CAST_STREAMS
    assert rq % BF16_ROWS == 0

    def slab(q):
        return pl.BlockSpec((None, rq, cols), lambda i: (j, i * CAST_STREAMS + q, 0))

    out = pl.pallas_call(
        _cast_kernel,
        grid=(rows // rb,),
        in_specs=[slab(q) for q in range(CAST_STREAMS)],
        out_specs=pl.BlockSpec((rb, cols), lambda i: (i, 0)),
        out_shape=jax.ShapeDtypeStruct((rows, cols), BF16),
        compiler_params=_cparams(("parallel",)),
        name="weights_to_bf16",
    )(*([w3] * CAST_STREAMS))
    return out.reshape(w.shape[1:])


def _mod_kernel(c_ref, w_ref, b_ref, o_ref):
    c = c_ref[...]
    s = (c * jax.nn.sigmoid(c)).astype(BF16)
    o_ref[...] = jnp.dot(s, w_ref[...].astype(BF16), preferred_element_type=F32) + b_ref[...]


def _modulation_table(cvecs, ada_w, ada_b):
    tn = 2048
    width = N_MOD * D_MODEL
    return pl.pallas_call(
        _mod_kernel,
        grid=(DEPTH, width // tn),
        in_specs=[pl.BlockSpec((MOD_ROWS, D_MODEL), lambda i, j: (0, 0)),
                  pl.BlockSpec((None, D_MODEL, tn), lambda i, j: (i, 0, j)),
                  pl.BlockSpec((None, 1, tn), lambda i, j: (i, 0, j))],
        out_specs=pl.BlockSpec((None, MOD_ROWS, tn), lambda i, j: (i, 0, j)),
        out_shape=jax.ShapeDtypeStruct((DEPTH, MOD_ROWS, width), F32),
        compiler_params=_cparams(("parallel", "parallel")),
        name="adaln_table",
    )(cvecs, ada_w, ada_b.reshape(DEPTH, 1, width))


def _prenorm_kernel(x_ref, g_ref, sh_ref, sc_ref, u_ref, h_scr):
    tm = x_ref.shape[0]
    r = tm // SSM_CHUNK
    h = _norm_mod(x_ref[...], g_ref[...], sh_ref[...], sc_ref[...])
    ncol = D_MODEL // LANES
    for c in range(ncol):
        h_scr[c] = h[:, c * LANES:(c + 1) * LANES]
    lane_q = lax.broadcasted_iota(jnp.int32, (r, LANES), 1) // SSM_PAIR
    for c in range(ncol):
        steps = [h_scr[c, pl.ds(k, r, stride=SSM_CHUNK), :] for k in range(SSM_CHUNK)]
        for pp in range(QUAD):
            cols = []
            for kc in range(SSM_CHUNK // QUAD):
                acc = None
                for jq in range(QUAD):
                    v = steps[QUAD * kc + jq]
                    shift = ((jq - pp) * SSM_PAIR) % LANES
                    if shift:
                        v = pltpu.roll(v, shift, 1)
                    acc = v if acc is None else jnp.where(lane_q == jq, v, acc)
                cols.append(acc)
            u_ref[QUAD * c + pp] = jnp.concatenate(cols, axis=1).astype(BF16)


def _prenorm(x, g, mod, bc, n, b):
    nt = x.shape[0]
    tm = ROW_TILE
    r = tm // SSM_CHUNK
    row = pl.BlockSpec((tm, D_MODEL), lambda t: (t, 0))
    return pl.pallas_call(
        _prenorm_kernel,
        grid=(nt // tm,),
        in_specs=[row, pl.BlockSpec((1, D_MODEL), lambda t: (0, 0)),
                  _mod_spec(0, tm, bc, n, b), _mod_spec(1, tm, bc, n, b)],
        out_specs=pl.BlockSpec((N_PAIRS, r, PAIR_W), lambda t: (0, t, 0)),
        out_shape=jax.ShapeDtypeStruct((N_PAIRS, nt // SSM_CHUNK, PAIR_W), BF16),
        scratch_shapes=[pltpu.VMEM((D_MODEL // LANES, tm, LANES), F32)],
        compiler_params=_cparams(("parallel",)),
        name="s5_prenorm",
    )(x, g, mod, mod)


def _cmul_add(xr, xi, lr, li, tr, ti):
    return xr + lr * tr - li * ti, xi + lr * ti + li * tr


def _block_scan(er, ei, tab_r, tab_i, s_r, s_i, row, reverse):
    xr, xi = er, ei
    for k in (1, 2, 4):
        if reverse:
            keep = row < SCAN_ROWS - k
            lr, li = tab_r[SCAN_ROWS - k:SCAN_ROWS - k + 1], tab_i[SCAN_ROWS - k:SCAN_ROWS - k + 1]
            shift = SCAN_ROWS - k
        else:
            keep = row >= k
            lr, li = tab_r[k - 1:k], tab_i[k - 1:k]
            shift = k
        tr = jnp.where(keep, pltpu.roll(xr, shift, 0), 0.0)
        ti = jnp.where(keep, pltpu.roll(xi, shift, 0), 0.0)
        xr, xi = _cmul_add(xr, xi, lr, li, tr, ti)
    ar, ai = _cmul_add(xr, xi, tab_r, tab_i, s_r, s_i)
    if reverse:
        edge = row == SCAN_ROWS - 1
        inr = jnp.where(edge, s_r, pltpu.roll(ar, SCAN_ROWS - 1, 0))
        ini = jnp.where(edge, s_i, pltpu.roll(ai, SCAN_ROWS - 1, 0))
        return inr, ini, ar[0:1], ai[0:1]
    edge = row == 0
    inr = jnp.where(edge, s_r, pltpu.roll(ar, 1, 0))
    ini = jnp.where(edge, s_i, pltpu.roll(ai, 1, 0))
    return inr, ini, ar[SCAN_ROWS - 1:SCAN_ROWS], ai[SCAN_ROWS - 1:SCAN_ROWS]


def _s5_core_kernel(u_ref, w3_ref, w2t_ref, quad_ref, tab_ref, y_ref, e_scr, s_scr, tz_scr, *,
                    b, n_ctx_chunks, n_chunks):
    for k in range(SSM_CHUNK):
        for q in range(SSM_CHUNK // QUAD):
            tz_scr[k * SSM_PAIR:(k + 1) * SSM_PAIR, q * LANES:(q + 1) * LANES] = (
                quad_ref[QUAD * q - k + SSM_CHUNK - 1])
    u = u_ref[...]
    e_scr[...] = jnp.dot(u, w3_ref[...], preferred_element_type=F32)
    q4 = LANES
    tfr, tfi, trr, tri = tab_ref[0], tab_ref[1], tab_ref[2], tab_ref[3]
    row = lax.broadcasted_iota(jnp.int32, (SCAN_ROWS, q4), 0)
    nb_ctx = n_ctx_chunks // SCAN_ROWS
    nb = n_chunks // SCAN_ROWS

    def body(m, carry):
        mr = jnp.where(m < nb_ctx, nb_ctx - 1 - m, nb - 1 - (m - nb_ctx))
        n_lat_chunks = n_chunks - n_ctx_chunks

        def block_rows(bi, blk):
            start = jnp.where(blk < nb_ctx, bi * n_ctx_chunks + blk * SCAN_ROWS,
                              b * n_ctx_chunks + bi * n_lat_chunks + (blk - nb_ctx) * SCAN_ROWS)
            return pl.ds(pl.multiple_of(start, SCAN_ROWS), SCAN_ROWS)

        out = []
        for bi in range(b):
            sfr, sfi, srr, sri = carry[4 * bi:4 * bi + 4]
            rf = block_rows(bi, m)
            rr = block_rows(bi, mr)
            inr, ini, sfr, sfi = _block_scan(e_scr[rf, 0:q4], e_scr[rf, q4:2 * q4], tfr, tfi, sfr, sfi, row, False)
            s_scr[rf, 0:q4] = inr
            s_scr[rf, q4:2 * q4] = ini
            inr, ini, srr, sri = _block_scan(e_scr[rr, 2 * q4:3 * q4], e_scr[rr, 3 * q4:4 * q4], trr, tri,
                                             srr, sri, row, True)
            s_scr[rr, 2 * q4:3 * q4] = inr
            s_scr[rr, 3 * q4:4 * q4] = ini
            out += [sfr, sfi, srr, sri]
        return tuple(out)

    z = jnp.zeros((1, q4), F32)
    lax.fori_loop(0, nb, body, (z,) * (4 * b))
    y = jnp.dot(u, tz_scr[...], preferred_element_type=F32)
    y = y + lax.dot_general(s_scr[...].astype(BF16), w2t_ref[...], (((1,), (1,)), ((), ())),
                            preferred_element_type=F32)
    y_ref[...] = y.astype(BF16)


def _s5_core(u, w3, w2t, quads, tab, b, n_ctx_chunks, n_chunks):
    rows = u.shape[1]
    mat = pl.BlockSpec((None, PAIR_W, PAIR_W), lambda p: (p, 0, 0))
    seq = pl.BlockSpec((None, rows, PAIR_W), lambda p: (p, 0, 0))
    return pl.pallas_call(
        functools.partial(_s5_core_kernel, b=b, n_ctx_chunks=n_ctx_chunks, n_chunks=n_chunks),
        grid=(N_PAIRS,),
        in_specs=[seq, mat, mat,
                  pl.BlockSpec((None, N_QUADS, SSM_PAIR, LANES), lambda p: (p, 0, 0, 0)),
                  pl.BlockSpec((None, 4, SCAN_ROWS, LANES), lambda p: (p, 0, 0, 0))],
        out_specs=seq,
        out_shape=jax.ShapeDtypeStruct((N_PAIRS, rows, PAIR_W), BF16),
        scratch_shapes=[pltpu.VMEM((rows, PAIR_W), F32), pltpu.VMEM((rows, PAIR_W), F32),
                        pltpu.VMEM((PAIR_W, PAIR_W), BF16)],
        compiler_params=_cparams(("parallel",)),
        name="s5_core",
    )(u, w3, w2t, quads, tab)


def _s5_out_kernel(x_ref, yp_ref, g_ref, sh_ref, sc_ref, gt_ref, d_ref, w_ref, wb_ref, o_ref, y_scr):
    x = x_ref[...]
    tm = x.shape[0]
    r = tm // SSM_CHUNK
    lane_q = lax.broadcasted_iota(jnp.int32, (r, LANES), 1) // SSM_PAIR
    ncol = D_MODEL // LANES
    for c in range(ncol):
        for kc in range(SSM_CHUNK // QUAD):
            src = [yp_ref[QUAD * c + pp, :, kc * LANES:(kc + 1) * LANES].astype(F32) for pp in range(QUAD)]
            for jq in range(QUAD):
                acc = None
                for pp in range(QUAD):
                    v = src[pp]
                    shift = ((pp - jq) * SSM_PAIR) % LANES
                    if shift:
                        v = pltpu.roll(v, shift, 1)
                    acc = v if acc is None else jnp.where(lane_q == pp, v, acc)
                y_scr[c, pl.ds(QUAD * kc + jq, r, stride=SSM_CHUNK), :] = acc
    y = jnp.concatenate([y_scr[c] for c in range(ncol)], axis=1)
    h = _norm_mod(x, g_ref[...], sh_ref[...], sc_ref[...])
    yy = y + d_ref[...] * h
    z = jax.nn.gelu(yy).astype(BF16)
    ag = jnp.dot(z, w_ref[...], preferred_element_type=F32) + wb_ref[...]
    a = ag[:, :D_MODEL]
    g = ag[:, D_MODEL:]
    o_ref[...] = x + gt_ref[...] * (a * jax.nn.sigmoid(g))


def _s5_out(x, yp, g, mod, d, glu_w, glu_b, bc, n, b):
    nt = x.shape[0]
    tm = ROW_TILE
    row = pl.BlockSpec((tm, D_MODEL), lambda t: (t, 0))
    vec = pl.BlockSpec((1, D_MODEL), lambda t: (0, 0))
    return pl.pallas_call(
        _s5_out_kernel,
        grid=(nt // tm,),
        in_specs=[row, pl.BlockSpec((N_PAIRS, tm // SSM_CHUNK, PAIR_W), lambda t: (0, t, 0)), vec,
                  _mod_spec(0, tm, bc, n, b), _mod_spec(1, tm, bc, n, b), _mod_spec(2, tm, bc, n, b),
                  vec,
                  pl.BlockSpec((D_MODEL, 2 * D_MODEL), lambda t: (0, 0)),
                  pl.BlockSpec((1, 2 * D_MODEL), lambda t: (0, 0))],
        out_specs=row,
        out_shape=jax.ShapeDtypeStruct((nt, D_MODEL), F32),
        scratch_shapes=[pltpu.VMEM((D_MODEL // LANES, tm, LANES), F32)],
        compiler_params=_cparams(("parallel",)),
        name="s5_out",
    )(x, yp, g, mod, mod, mod, d, glu_w, glu_b)


def _pair_lanes(x):
    z = jnp.zeros_like(x[..., 0, :, :])
    top = jnp.concatenate([x[..., 0, :, :], z], axis=-1)
    bot = jnp.concatenate([z, x[..., 1, :, :]], axis=-1)
    return jnp.stack([top, bot], axis=-3)


def _s5_chunk_weights(a_re, a_im, log_dt, b_re, b_im, c_re, c_im):
    L = SSM_CHUNK
    G, P, I = SSM_GROUPS, SSM_STATE, SSM_GROUP
    a_re = a_re.astype(F32)
    a_im = a_im.astype(F32)
    dt = jnp.exp(log_dt.astype(F32))[..., None]
    mag = jnp.exp(a_re * dt)
    lam_re = mag * jnp.cos(a_im * dt)
    lam_im = mag * jnp.sin(a_im * dt)
    den = a_re * a_re + a_im * a_im
    num_re = lam_re - 1.0
    f_re = (num_re * a_re + lam_im * a_im) / den
    f_im = (lam_im * a_re - num_re * a_im) / den
    b_re = b_re.astype(F32)
    b_im = b_im.astype(F32)
    bb_re = f_re[..., None] * b_re - f_im[..., None] * b_im
    bb_im = f_re[..., None] * b_im + f_im[..., None] * b_re
    m = jnp.arange(L + 1, dtype=F32)[:, None, None, None]
    pmag = jnp.exp(m * (a_re * dt)[None])
    pw_re = pmag * jnp.cos(m * (a_im * dt)[None])
    pw_im = pmag * jnp.sin(m * (a_im * dt)[None])
    c_re = c_re.astype(F32)
    c_im = c_im.astype(F32)
    cl_re = c_re[None] * pw_re[:, :, :, None, :] - c_im[None] * pw_im[:, :, :, None, :]
    cl_im = c_re[None] * pw_im[:, :, :, None, :] + c_im[None] * pw_re[:, :, :, None, :]
    bt_re = jnp.swapaxes(bb_re, -1, -2)
    bt_im = jnp.swapaxes(bb_im, -1, -2)
    lb_re = pw_re[:, :, :, None, :] * bt_re[None] - pw_im[:, :, :, None, :] * bt_im[None]
    lb_im = pw_re[:, :, :, None, :] * bt_im[None] + pw_im[:, :, :, None, :] * bt_re[None]

    def pair_rows(parts):
        cols = []
        for x in parts:
            x = x.reshape(L, N_PAIRS, 2, x.shape[-2], P)
            cols.append(_pair_lanes(jnp.transpose(x, (1, 0, 2, 3, 4))))
        return jnp.concatenate(cols, axis=-1).reshape(N_PAIRS, PAIR_W, 4 * 2 * P)

    pf = (L - 1) - jnp.arange(L)
    pr = jnp.arange(L)
    w3 = pair_rows([lb_re[pf, 0], lb_im[pf, 0], lb_re[pr, 1], lb_im[pr, 1]])
    qf = jnp.arange(L) + 1
    qr = L - jnp.arange(L)
    w2t = pair_rows([cl_re[qf, 0], -cl_im[qf, 0], cl_re[qr, 1], -cl_im[qr, 1]])
    kk = (jnp.sum(cl_re[:L, :, :, None, :, :] * bt_re[None, :, :, :, None, :], axis=-1)
          - jnp.sum(cl_im[:L, :, :, None, :, :] * bt_im[None, :, :, :, None, :], axis=-1))
    kp = _pair_lanes(kk.reshape(L, 2, N_PAIRS, 2, I, I)).reshape(L, 2, N_PAIRS, SSM_PAIR, SSM_PAIR)
    lags = jnp.concatenate([kp[:0:-1, 1], kp[:1, 0] + kp[:1, 1], kp[1:, 0]], axis=0)
    quads = jnp.concatenate([lags[q:q + N_QUADS] for q in range(QUAD)], axis=-1)
    quads = jnp.transpose(quads, (1, 0, 2, 3))
    jf = (jnp.arange(SCAN_ROWS, dtype=F32) + 1.0) * L
    jr = (SCAN_ROWS - jnp.arange(SCAN_ROWS, dtype=F32)) * L
    jj = jnp.stack([jf, jr], axis=0)[:, :, None, None]
    tmag = jnp.exp(jj * (a_re * dt)[:, None])
    t_re = tmag * jnp.cos(jj * (a_im * dt)[:, None])
    t_im = tmag * jnp.sin(jj * (a_im * dt)[:, None])
    tab = jnp.stack([t_re[0], t_im[0], t_re[1], t_im[1]], axis=0)
    tab = jnp.transpose(tab.reshape(4, SCAN_ROWS, N_PAIRS, 2 * P), (2, 0, 1, 3))
    return w3.astype(BF16), w2t.astype(BF16), quads.astype(BF16), tab


def _s5_layer(x, g, mod, ssm, bc, n, b):
    (a_re, a_im, log_dt, b_re, b_im, c_re, c_im, d, glu_w, glu_b) = ssm
    c_len = bc // b
    n_ctx_chunks = c_len // SSM_CHUNK
    n_chunks = (c_len + n) // SSM_CHUNK
    u = _prenorm(x, g, mod, bc, n, b)
    w3, w2t, quads, tab = _s5_chunk_weights(a_re, a_im, log_dt, b_re, b_im, c_re, c_im)
    yp = _s5_core(u, w3, w2t, quads, tab, b, n_ctx_chunks, n_chunks)
    return _s5_out(x, yp, g, mod, d.reshape(1, D_MODEL), glu_w.astype(BF16),
                   glu_b.reshape(1, 2 * D_MODEL), bc, n, b)


def _ffn_kernel(x_ref, g_ref, sh_ref, sc_ref, gt_ref, wg_ref, wu_ref, wd_ref, o_ref, h_scr, acc_scr):
    j = pl.program_id(1)

    @pl.when(j == 0)
    def _():
        h_scr[...] = _norm_mod(x_ref[...], g_ref[...], sh_ref[...], sc_ref[...]).astype(BF16)
        acc_scr[...] = jnp.zeros_like(acc_scr)

    h = h_scr[...]
    gate = jnp.dot(h, wg_ref[...], preferred_element_type=F32)
    up = jnp.dot(h, wu_ref[...], preferred_element_type=F32)
    a = (gate * jax.nn.sigmoid(gate) * up).astype(BF16)
    acc_scr[...] += jnp.dot(a, wd_ref[...], preferred_element_type=F32)

    @pl.when(j == pl.num_programs(1) - 1)
    def _():
        o_ref[...] = x_ref[...] + gt_ref[...] * acc_scr[...]


def _ffn(x, g, mod, w_gate_up, w_down, bc, n, b):
    nt = x.shape[0]
    tm = ROW_TILE
    nj = D_FF // FF_TILE
    row = pl.BlockSpec((tm, D_MODEL), lambda t, j: (t, 0))
    return pl.pallas_call(
        _ffn_kernel,
        grid=(nt // tm, nj),
        in_specs=[row, pl.BlockSpec((1, D_MODEL), lambda t, j: (0, 0)),
                  _mod_spec(3, tm, bc, n, b), _mod_spec(4, tm, bc, n, b), _mod_spec(5, tm, bc, n, b),
                  pl.BlockSpec((D_MODEL, FF_TILE), lambda t, j: (0, j)),
                  pl.BlockSpec((D_MODEL, FF_TILE), lambda t, j: (0, nj + j)),
                  pl.BlockSpec((FF_TILE, D_MODEL), lambda t, j: (j, 0))],
        out_specs=row,
        out_shape=jax.ShapeDtypeStruct((nt, D_MODEL), F32),
        scratch_shapes=[pltpu.VMEM((tm, D_MODEL), BF16), pltpu.VMEM((tm, D_MODEL), F32)],
        compiler_params=_cparams(("parallel", "arbitrary")),
        name="ffn_dense",
    )(x, g, mod, mod, mod, w_gate_up, w_gate_up, w_down)


def _head_ms(sq, ones):
    return jnp.dot(sq.astype(BF16), ones, preferred_element_type=F32)


def _rope(v, cos, sin_signed, first_half):
    w = v.shape[-1]
    quarter = HEAD_DIM // 4
    swapped = jnp.where(first_half, pltpu.roll(v, w - quarter, 1), pltpu.roll(v, quarter, 1))
    return v * cos + swapped * sin_signed


def _qkv_kernel(x_ref, g_ref, sh_ref, sc_ref, w_ref, qg_ref, kg_ref, ones_ref, cos_ref, sin_ref,
                q_ref, k_ref, v_ref):
    tm = x_ref.shape[0]
    h = _norm_mod(x_ref[...], g_ref[...], sh_ref[...], sc_ref[...]).astype(BF16)
    qkv = jnp.dot(h, w_ref[...], preferred_element_type=F32)
    dq = N_Q_HEADS * HEAD_DIM
    dk = N_KV_HEADS * HEAD_DIM
    q = qkv[:, :dq]
    k = qkv[:, dq:dq + dk]
    v = qkv[:, dq + dk:]
    ones = ones_ref[...]
    q = q * lax.rsqrt(_head_ms(q * q, ones) + EPS) * qg_ref[...]
    k = k * lax.rsqrt(_head_ms(k * k, ones[:dk, :dk]) + EPS) * kg_ref[...]
    cos = cos_ref[...]
    sin = sin_ref[...]
    lane_q = lax.broadcasted_iota(jnp.int32, (tm, dq), 1)
    lane_k = lax.broadcasted_iota(jnp.int32, (tm, dk), 1)
    half = HEAD_DIM // 2
    quarter = HEAD_DIM // 4
    q = _rope(q, jnp.tile(cos, (1, dq // LANES)), jnp.tile(sin, (1, dq // LANES)), (lane_q % half) < quarter)
    k = _rope(k, jnp.tile(cos, (1, dk // LANES)), jnp.tile(sin, (1, dk // LANES)), (lane_k % half) < quarter)
    q = q * (HEAD_DIM ** -0.5)
    k_ref[...] = k.astype(BF16)
    v_ref[...] = v.astype(BF16)
    lane = lax.broadcasted_iota(jnp.int32, (tm, LANES), 1)
    low = lane < HEAD_DIM
    for c in range(dq // LANES):
        vc = q[:, c * LANES:(c + 1) * LANES]
        vr = pltpu.roll(vc, HEAD_DIM, 1)
        kv_half = (c // 2) % 2
        if kv_half == 0:
            first = jnp.where(low, vc, 0.0)
            second = jnp.where(low, vr, 0.0)
        else:
            first = jnp.where(low, 0.0, vr)
            second = jnp.where(low, 0.0, vc)
        q_ref[:, (2 * c) * LANES:(2 * c + 1) * LANES] = first.astype(BF16)
        q_ref[:, (2 * c + 1) * LANES:(2 * c + 2) * LANES] = second.astype(BF16)


def _qkv(x, g, mod, w_qkv, q_g, k_g, ones, cos, sin, bc, n, b):
    nt = x.shape[0]
    tm = ROW_TILE
    dq = N_Q_HEADS * HEAD_DIM
    dk = N_KV_HEADS * HEAD_DIM
    row = lambda w: pl.BlockSpec((tm, w), lambda t: (t, 0))
    full = lambda r, w: pl.BlockSpec((r, w), lambda t: (0, 0))
    return pl.pallas_call(
        _qkv_kernel,
        grid=(nt // tm,),
        in_specs=[row(D_MODEL), full(1, D_MODEL), _mod_spec(0, tm, bc, n, b), _mod_spec(1, tm, bc, n, b),
                  full(D_MODEL, dq + 2 * dk), full(1, dq), full(1, dk), full(dq, dq),
                  row(LANES), row(LANES)],
        out_specs=[row(2 * dq), row(dk), row(dk)],
        out_shape=[jax.ShapeDtypeStruct((nt, 2 * dq), BF16),
                   jax.ShapeDtypeStruct((nt, dk), BF16),
                   jax.ShapeDtypeStruct((nt, dk), BF16)],
        compiler_params=_cparams(("parallel",)),
        name="attn_qkv",
    )(x, g, mod, mod, w_qkv, q_g, k_g, ones, cos, sin)


def _attn_kernel(*refs, sub, n_sub, n_lat):
    if n_lat:
        q_ref, kc_ref, vc_ref, kl_ref, vl_ref, o_ref = refs[:6]
    else:
        q_ref, kc_ref, vc_ref, o_ref = refs[:4]
    s_bufs = refs[-2 * ATTN_DEPTH:-ATTN_DEPTH]
    m_bufs = refs[-ATTN_DEPTH:]
    grp = pl.program_id(1)
    nt_dims = (((1,), (1,)), ((), ()))
    c_len = kc_ref.shape[0]
    parts = [(kc_ref, vc_ref, 0, 0, c_len)]
    if n_lat:
        kc = min(ATTN_KEY_CHUNK, n_lat)
        parts += [(kl_ref, vl_ref, c0, c_len + c0, kc) for c0 in range(0, n_lat, kc)]
    lane = lax.broadcasted_iota(jnp.int32, (sub, LANES), 1)
    low = lane < HEAD_DIM
    upper_half = (grp % 2) == 1

    def scores(i, s_scr, m_scr):
        r0 = pl.multiple_of(i * sub, sub)
        qs = jnp.concatenate([q_ref[pl.ds(r0, sub), a * LANES:(a + 1) * LANES] for a in range(Q_PER_KV)],
                             axis=0)
        m = None
        for kr, _, k0, off, sz in parts:
            s = lax.dot_general(qs, kr[k0:k0 + sz, :], nt_dims, preferred_element_type=F32)
            s_scr[:, off:off + sz] = s
            mx = s.max(axis=-1, keepdims=True)
            m = mx if m is None else jnp.maximum(m, mx)
        m_scr[...] = m

    def output(i, s_scr, m_scr):
        r0 = pl.multiple_of(i * sub, sub)
        m = m_scr[...]
        l = jnp.zeros_like(m)
        acc = jnp.zeros((Q_PER_KV * sub, LANES), F32)
        for _, vr, k0, off, sz in parts:
            p = jnp.exp(s_scr[:, off:off + sz] - m)
            l = l + p.sum(axis=-1, keepdims=True)
            acc = acc + jnp.dot(p.astype(BF16), vr[k0:k0 + sz, :], preferred_element_type=F32)
        o = acc / l
        for c in range(Q_PER_KV // 2):
            a0 = o[(2 * c) * sub:(2 * c + 1) * sub]
            a1 = o[(2 * c + 1) * sub:(2 * c + 2) * sub]
            from_low = jnp.where(low, a0, pltpu.roll(a1, HEAD_DIM, 1))
            from_high = jnp.where(low, pltpu.roll(a0, HEAD_DIM, 1), a1)
            o_ref[pl.ds(r0, sub), c * LANES:(c + 1) * LANES] = (
                jnp.where(upper_half, from_high, from_low).astype(BF16))

    depth = ATTN_DEPTH
    ahead = depth - 1

    def stage(i, r):
        output(i, s_bufs[r], m_bufs[r])
        nxt = (r + ahead) % depth
        scores(i + ahead, s_bufs[nxt], m_bufs[nxt])

    for t in range(min(ahead, n_sub)):
        scores(t, s_bufs[t], m_bufs[t])
    n_loop = max(n_sub - ahead, 0) // depth

    def body(j, carry):
        for r in range(depth):
            stage(j * depth + r, r)
        return carry

    if n_loop:
        lax.fori_loop(0, n_loop, body, 0)
    for i in range(n_loop * depth, n_sub):
        if i + ahead < n_sub:
            stage(i, i % depth)
        else:
            output(i, s_bufs[i % depth], m_bufs[i % depth])


def _attn_call(name, tq, n_lat, c_len, grid, in_specs, out_spec, out_rows, args):
    rows = Q_PER_KV * ATTN_SUB
    assert tq % ATTN_SUB == 0
    return pl.pallas_call(
        functools.partial(_attn_kernel, sub=ATTN_SUB, n_sub=tq // ATTN_SUB, n_lat=n_lat),
        grid=grid,
        in_specs=in_specs,
        out_specs=out_spec,
        out_shape=jax.ShapeDtypeStruct((out_rows, D_MODEL), BF16),
        scratch_shapes=([pltpu.VMEM((rows, c_len + n_lat), F32)] * ATTN_DEPTH
                        + [pltpu.VMEM((rows, 1), F32)] * ATTN_DEPTH),
        compiler_params=_cparams(("parallel", "parallel", "parallel")),
        name=name,
    )(*args)


def _attn_latent(qz, k, v, bc, n, b):
    c_len = bc // b
    tq = min(Q_TILE, n)
    assert bc % tq == 0 and n % tq == 0
    gw = Q_PER_KV * LANES
    k_lat, v_lat = k[bc:], v[bc:]
    q_spec = pl.BlockSpec((tq, gw), lambda bi, g, qi: ((bc + bi * n) // tq + qi, g))
    ctx_spec = pl.BlockSpec((c_len, LANES), lambda bi, g, qi: (bi, g // 2))
    lat_spec = pl.BlockSpec((n, LANES), lambda bi, g, qi: (bi, g // 2))
    o_spec = pl.BlockSpec((tq, Q_PER_KV * HEAD_DIM), lambda bi, g, qi: (bi * (n // tq) + qi, g))
    return _attn_call("attn_latent", tq, n, c_len, (b, N_KV_HEADS, n // tq),
                      [q_spec, ctx_spec, ctx_spec, lat_spec, lat_spec], o_spec, b * n,
                      (qz, k, v, k_lat, v_lat))


def _attn_ctx(qz, k, v, bc, b):
    c_len = bc // b
    gw = Q_PER_KV * LANES
    q_spec = pl.BlockSpec((c_len, gw), lambda bi, g, qi: (bi, g))
    ctx_spec = pl.BlockSpec((c_len, LANES), lambda bi, g, qi: (bi, g // 2))
    o_spec = pl.BlockSpec((c_len, Q_PER_KV * HEAD_DIM), lambda bi, g, qi: (bi, g))
    return _attn_call("attn_ctx", c_len, 0, c_len, (b, N_KV_HEADS, 1),
                      [q_spec, ctx_spec, ctx_spec], o_spec, bc, (qz, k, v))


def _attn_out_kernel(o_ref, x_ref, wo_ref, gt_ref, g_ref, sh_ref, sc_ref, rw_ref, rb_ref,
                     xo_ref, h_ref, r_ref):
    tm = x_ref.shape[0]
    x = x_ref[...] + gt_ref[...] * jnp.dot(o_ref[...], wo_ref[...], preferred_element_type=F32)
    xo_ref[...] = x
    h = _norm_mod(x, g_ref[...], sh_ref[...], sc_ref[...])
    h_ref[...] = h
    logits = jnp.dot(h, rw_ref[...], preferred_element_type=F32, precision=lax.Precision.HIGHEST) + rb_ref[...]
    lane = lax.broadcasted_iota(jnp.int32, (tm, LANES), 1)
    neg = -jnp.inf
    logits = jnp.where(lane < N_EXPERTS, logits, neg)
    m1 = jnp.max(logits, axis=-1, keepdims=True)
    i1 = jnp.min(jnp.where(logits == m1, lane, LANES), axis=-1, keepdims=True)
    rest = jnp.where(lane == i1, neg, logits)
    m2 = jnp.max(rest, axis=-1, keepdims=True)
    i2 = jnp.min(jnp.where(rest == m2, lane, LANES), axis=-1, keepdims=True)
    e = jnp.exp(m2 - m1)
    w1 = 1.0 / (1.0 + e)
    w2 = e / (1.0 + e)
    r = jnp.where(lane == 0, i1.astype(F32), 0.0)
    r = jnp.where(lane == 1, i2.astype(F32), r)
    r = jnp.where(lane == 2, w1, r)
    r = jnp.where(lane == 3, w2, r)
    r_ref[...] = r


def _attn_out(o, x, w_o, g, mod, router_w, router_b, bc, n, b):
    nt = x.shape[0]
    tm = ROW_TILE
    row = lambda w: pl.BlockSpec((tm, w), lambda t: (t, 0))
    full = lambda r, w: pl.BlockSpec((r, w), lambda t: (0, 0))
    return pl.pallas_call(
        _attn_out_kernel,
        grid=(nt // tm,),
        in_specs=[row(D_MODEL), row(D_MODEL), full(D_MODEL, D_MODEL), _mod_spec(2, tm, bc, n, b),
                  full(1, D_MODEL), _mod_spec(3, tm, bc, n, b), _mod_spec(4, tm, bc, n, b),
                  full(D_MODEL, LANES), full(1, LANES)],
        out_specs=[row(D_MODEL), row(D_MODEL), row(LANES)],
        out_shape=[jax.ShapeDtypeStruct((nt, D_MODEL), F32),
                   jax.ShapeDtypeStruct((nt, D_MODEL), F32),
                   jax.ShapeDtypeStruct((nt, LANES), F32)],
        compiler_params=_cparams(("parallel",)),
        name="attn_out_router",
    )(o, x, w_o, mod, g, mod, mod, router_w, router_b)


def _moe_ffn_kernel(te_ref, tv_ref, x_ref, wg_ref, wu_ref, wd_ref, o_ref, h_scr, acc_scr):
    i = pl.program_id(0)
    j = pl.program_id(1)
    valid = tv_ref[i] > 0

    @pl.when(j == 0)
    def _():
        h_scr[...] = x_ref[...].astype(BF16)
        acc_scr[...] = jnp.zeros_like(acc_scr)

    @pl.when(valid)
    def _():
        h = h_scr[...]
        gate = jnp.dot(h, wg_ref[...], preferred_element_type=F32)
        up = jnp.dot(h, wu_ref[...], preferred_element_type=F32)
        a = (gate * jax.nn.sigmoid(gate) * up).astype(BF16)
        acc_scr[...] += jnp.dot(a, wd_ref[...], preferred_element_type=F32)

    @pl.when(j == pl.num_programs(1) - 1)
    def _():
        o_ref[...] = acc_scr[...].astype(BF16)


def _moe_ffn(xs, tile_expert, tile_valid, w_gate_up, w_down):
    slots = xs.shape[0]
    tm = MOE_TILE
    nj = D_FF // FF_TILE
    row = pl.BlockSpec((tm, D_MODEL), lambda i, j, te, tv: (i, 0))
    grid_spec = pltpu.PrefetchScalarGridSpec(
        num_scalar_prefetch=2,
        grid=(slots // tm, nj),
        in_specs=[row,
                  pl.BlockSpec((None, D_MODEL, FF_TILE), lambda i, j, te, tv: (te[i], 0, j)),
                  pl.BlockSpec((None, D_MODEL, FF_TILE), lambda i, j, te, tv: (te[i], 0, nj + j)),
                  pl.BlockSpec((None, FF_TILE, D_MODEL), lambda i, j, te, tv: (te[i], j, 0))],
        out_specs=row,
        scratch_shapes=[pltpu.VMEM((tm, D_MODEL), BF16), pltpu.VMEM((tm, D_MODEL), F32)])
    return pl.pallas_call(
        _moe_ffn_kernel,
        grid_spec=grid_spec,
        out_shape=jax.ShapeDtypeStruct((slots, D_MODEL), BF16),
        compiler_params=_cparams(("parallel", "arbitrary")),
        name="moe_ffn",
    )(tile_expert, tile_valid, xs, w_gate_up, w_gate_up, w_down)


def _moe_combine_kernel(x_ref, a_ref, b_ref, r_ref, gt_ref, o_ref):
    tm = x_ref.shape[0]
    lane = lax.broadcasted_iota(jnp.int32, (tm, LANES), 1)
    r = r_ref[...]
    w1 = jnp.sum(jnp.where(lane == 2, r, 0.0), axis=-1, keepdims=True)
    w2 = jnp.sum(jnp.where(lane == 3, r, 0.0), axis=-1, keepdims=True)
    y = w1 * a_ref[...].astype(F32) + w2 * b_ref[...].astype(F32)
    o_ref[...] = x_ref[...] + gt_ref[...] * y


def _moe_combine(x, ya, yb, route, mod, bc, n, b):
    nt = x.shape[0]
    tm = ROW_TILE
    row = lambda w: pl.BlockSpec((tm, w), lambda t: (t, 0))
    return pl.pallas_call(
        _moe_combine_kernel,
        grid=(nt // tm,),
        in_specs=[row(D_MODEL), row(D_MODEL), row(D_MODEL), row(LANES), _mod_spec(5, tm, bc, n, b)],
        out_specs=row(D_MODEL),
        out_shape=jax.ShapeDtypeStruct((nt, D_MODEL), F32),
        compiler_params=_cparams(("parallel",)),
        name="moe_combine",
    )(x, ya, yb, route, mod)


def _moe_layer(x, h, route, mod, w_gate_up, w_down, bc, n, b):
    nt = x.shape[0]
    tm = MOE_TILE
    top_i = route[:, :TOP_K].astype(jnp.int32)
    flat_e = top_i.reshape(-1)
    onehot = (flat_e[:, None] == jnp.arange(N_EXPERTS)[None, :]).astype(jnp.int32)
    counts = jnp.sum(onehot, axis=0)
    rank = jnp.sum((jnp.cumsum(onehot, axis=0) - onehot) * onehot, axis=1)
    padded = ((counts + tm - 1) // tm) * tm
    ends = jnp.cumsum(padded)
    starts = ends - padded
    pos = starts[flat_e] + rank
    n_tiles = (TOP_K * nt) // tm + N_EXPERTS
    slots = n_tiles * tm
    token_of_slot = jnp.zeros((slots,), jnp.int32).at[pos].set(jnp.arange(TOP_K * nt, dtype=jnp.int32) // TOP_K)
    tile_start = jnp.arange(n_tiles, dtype=jnp.int32) * tm
    tile_expert = jnp.sum((tile_start[:, None] >= ends[None, :]).astype(jnp.int32), axis=1)
    tile_valid = (tile_start < ends[-1]).astype(jnp.int32)
    last_expert = jnp.max(jnp.where(counts > 0, jnp.arange(N_EXPERTS), 0))
    tile_expert = jnp.where(tile_valid > 0, tile_expert, last_expert).astype(jnp.int32)
    xs = h.at[token_of_slot].get(mode="promise_in_bounds")
    ys = _moe_ffn(xs, tile_expert, tile_valid, w_gate_up, w_down)
    pos2 = pos.reshape(nt, TOP_K)
    ya = ys.at[pos2[:, 0]].get(mode="promise_in_bounds")
    yb = ys.at[pos2[:, 1]].get(mode="promise_in_bounds")
    return _moe_combine(x, ya, yb, route, mod, bc, n, b)


def _rope_tables(bc, n):
    rows = n // GRID_W
    row = jnp.broadcast_to(jnp.arange(rows)[:, None], (rows, GRID_W)).reshape(-1).astype(F32)
    col = jnp.broadcast_to(jnp.arange(GRID_W)[None, :], (rows, GRID_W)).reshape(-1).astype(F32)
    axis_dim = HEAD_DIM // 2
    inv = ROPE_THETA ** (-jnp.arange(0, axis_dim, 2, dtype=F32) / axis_dim)
    ar = row[:, None] * inv
    ac = col[:, None] * inv
    cos = jnp.concatenate([jnp.cos(ar), jnp.cos(ar), jnp.cos(ac), jnp.cos(ac)], axis=1)
    sin = jnp.concatenate([-jnp.sin(ar), jnp.sin(ar), -jnp.sin(ac), jnp.sin(ac)], axis=1)
    cos = jnp.tile(cos, (1, LANES // HEAD_DIM))
    sin = jnp.tile(sin, (1, LANES // HEAD_DIM))
    return cos, sin


def kernel(x, c, ctx, c_ctx, ada_w, ada_b, norm_mix_g, norm_ffn_g, ssm_a_re, ssm_a_im, ssm_log_dt,
           ssm_b_re, ssm_b_im, ssm_c_re, ssm_c_im, ssm_d, ssm_glu_w, ssm_glu_b, attn_w_qkv, attn_q_g,
           attn_k_g, attn_w_o, ffn_w_gate_up, ffn_w_down, moe_router_w, moe_router_b, moe_w_gate_up,
           moe_w_down):
    b, n, _ = x.shape
    c_len = ctx.shape[1]
    bc = b * c_len
    assert b + 1 <= MOD_ROWS and bc % ROW_TILE == 0 and n % ROW_TILE == 0
    assert c_len % SSM_CHUNK == 0 and n % SSM_CHUNK == 0 and n % GRID_W == 0

    cvecs = jnp.zeros((MOD_ROWS, D_MODEL), F32).at[:b].set(c).at[b].set(c_ctx)
    mods = _modulation_table(cvecs, ada_w, ada_b).reshape(DEPTH, MOD_ROWS, 1, N_MOD * D_MODEL)

    xs = jnp.concatenate([ctx.reshape(bc, D_MODEL), x.reshape(b * n, D_MODEL)], axis=0)

    cos_l, sin_l = _rope_tables(bc, n)
    cos = jnp.concatenate([jnp.ones((bc, LANES), F32), jnp.tile(cos_l, (b, 1))], axis=0)
    sin = jnp.concatenate([jnp.zeros((bc, LANES), F32), jnp.tile(sin_l, (b, 1))], axis=0)
    dq = N_Q_HEADS * HEAD_DIM
    hid = jnp.arange(dq) // HEAD_DIM
    ones = ((hid[:, None] == hid[None, :]).astype(F32) / HEAD_DIM).astype(BF16)

    for i in range(DEPTH):
        j = i // 2
        last = i == DEPTH - 1
        mod = mods[i]
        g_mix = norm_mix_g[i].reshape(1, D_MODEL)
        g_ffn = norm_ffn_g[i].reshape(1, D_MODEL)
        if i % 2 == 0:
            ssm = (ssm_a_re[j], ssm_a_im[j], ssm_log_dt[j], ssm_b_re[j], ssm_b_im[j], ssm_c_re[j],
                   ssm_c_im[j], ssm_d[j], ssm_glu_w[j], ssm_glu_b[j])
            xs = _s5_layer(xs, g_mix, mod, ssm, bc, n, b)
            xs = _ffn(xs, g_ffn, mod, _to_bf16(ffn_w_gate_up, j), _to_bf16(ffn_w_down, j), bc, n, b)
        else:
            qz, k, v = _qkv(xs, g_mix, mod, attn_w_qkv[j].astype(BF16),
                            jnp.tile(attn_q_g[j], N_Q_HEADS).reshape(1, dq),
                            jnp.tile(attn_k_g[j], N_KV_HEADS).reshape(1, N_KV_HEADS * HEAD_DIM),
                            ones, cos, sin, bc, n, b)
            o = _attn_latent(qz, k, v, bc, n, b)
            if last:
                xs = xs[bc:]
                lay = (0, n, b)
            else:
                o = jnp.concatenate([_attn_ctx(qz, k, v, bc, b), o], axis=0)
                lay = (bc, n, b)
            rw = jnp.zeros((D_MODEL, LANES), F32).at[:, :N_EXPERTS].set(moe_router_w[j])
            rb = jnp.zeros((1, LANES), F32).at[0, :N_EXPERTS].set(moe_router_b[j])
            xs, h, route = _attn_out(o, xs, attn_w_o[j].astype(BF16), g_ffn, mod, rw, rb, *lay)
            xs = _moe_layer(xs, h, route, mod, _to_bf16(moe_w_gate_up, j), _to_bf16(moe_w_down, j), *lay)
    return xs.reshape(b, n, D_MODEL)
```

```python
import functools
import math

import jax
import jax.numpy as jnp
from jax import lax
from jax.experimental import pallas as pl
from jax.experimental.pallas import tpu as pltpu

F32 = jnp.float32
BF16 = jnp.bfloat16

D_MODEL = 1024
DEPTH = 4
GRID_W = 64
SSM_GROUP = 16
SSM_GROUPS = D_MODEL // SSM_GROUP
SSM_STATE = 64
HEAD_DIM = 64
N_Q_HEADS = D_MODEL // HEAD_DIM
N_KV_HEADS = 4
Q_PER_KV = N_Q_HEADS // N_KV_HEADS
ROPE_THETA = 10000.0
D_FF = 2816
N_EXPERTS = 8
TOP_K = 2
N_MOD = 6
EPS = 1e-6

LANES = 128
MOD_ROWS = 8
BF16_ROWS = 16
CAST_BLOCK_BYTES = 8 * 1024 * 1024
CAST_STREAMS = 1

ROW_TILE = 512
FF_TILE = 1408
MOE_TILE = 512
SSM_CHUNK = 16
SSM_PAIR = 2 * SSM_GROUP
N_PAIRS = D_MODEL // SSM_PAIR
SCAN_ROWS = 8
QUAD = LANES // SSM_PAIR
N_QUADS = 2 * SSM_CHUNK - QUAD
PAIR_W = SSM_CHUNK * SSM_PAIR
Q_TILE = 1024
ATTN_SUB = 64
ATTN_DEPTH = 3
ATTN_KEY_CHUNK = 1024
VMEM_LIMIT = 56 * 1024 * 1024


def _cparams(sem):
    return pltpu.CompilerParams(dimension_semantics=sem, vmem_limit_bytes=VMEM_LIMIT)


def _row_class(t, tile, bc, n, b):
    r0 = t * tile
    return jnp.where(r0 < bc, b, (r0 - bc) // max(n, 1))


def _mod_spec(piece, tile, bc, n, b):
    return pl.BlockSpec((None, 1, D_MODEL),
                        lambda t, *_: (_row_class(t, tile, bc, n, b), 0, piece))


def _norm_mod(x, g, shift, scale):
    ms = jnp.mean(x * x, axis=-1, keepdims=True)
    return (x * lax.rsqrt(ms + EPS) * g) * (1.0 + scale) + shift


def _cast_kernel(*refs):
    o_ref = refs[-1]
    rq = refs[0].shape[0]
    for q, w_ref in enumerate(refs[:-1]):
        o_ref[q * rq:(q + 1) * rq, :] = w_ref[...].astype(BF16)


def _to_bf16(w, j):
    cols = w.shape[-1]
    w3 = w.reshape(w.shape[0], -1, cols)
    rows = w3.shape[1]
    rb = 1 << ((CAST_BLOCK_BYTES // (4 * cols)).bit_length() - 1)
    while rows % rb:
        rb //= 2
    rq = rb // CAST_STREAMS
    assert rq % BF16_ROWS == 0

    def slab(q):
        return pl.BlockSpec((None, rq, cols), lambda i: (j, i * CAST_STREAMS + q, 0))

    out = pl.pallas_call(
        _cast_kernel,
        grid=(rows // rb,),
        in_specs=[slab(q) for q in range(CAST_STREAMS)],
        out_specs=pl.BlockSpec((rb, cols), lambda i: (i, 0)),
        out_shape=jax.ShapeDtypeStruct((rows, cols), BF16),
        compiler_params=_cparams(("parallel",)),
        name="weights_to_bf16",
    )(*([w3] * CAST_STREAMS))
    return out.reshape(w.shape[1:])


def _mod_kernel(c_ref, w_ref, b_ref, o_ref):
    c = c_ref[...]
    s = (c * jax.nn.sigmoid(c)).astype(BF16)
    o_ref[...] = jnp.dot(s, w_ref[...].astype(BF16), preferred_element_type=F32) + b_ref[...]


def _modulation_table(cvecs, ada_w, ada_b):
    tn = 2048
    width = N_MOD * D_MODEL
    return pl.pallas_call(
        _mod_kernel,
        grid=(DEPTH, width // tn),
        in_specs=[pl.BlockSpec((MOD_ROWS, D_MODEL), lambda i, j: (0, 0)),
                  pl.BlockSpec((None, D_MODEL, tn), lambda i, j: (i, 0, j)),
                  pl.BlockSpec((None, 1, tn), lambda i, j: (i, 0, j))],
        out_specs=pl.BlockSpec((None, MOD_ROWS, tn), lambda i, j: (i, 0, j)),
        out_shape=jax.ShapeDtypeStruct((DEPTH, MOD_ROWS, width), F32),
        compiler_params=_cparams(("parallel", "parallel")),
        name="adaln_table",
    )(cvecs, ada_w, ada_b.reshape(DEPTH, 1, width))


def _prenorm_kernel(x_ref, g_ref, sh_ref, sc_ref, u_ref, h_scr):
    tm = x_ref.shape[0]
    r = tm // SSM_CHUNK
    h = _norm_mod(x_ref[...], g_ref[...], sh_ref[...], sc_ref[...])
    ncol = D_MODEL // LANES
    for c in range(ncol):
        h_scr[c] = h[:, c * LANES:(c + 1) * LANES]
    lane_q = lax.broadcasted_iota(jnp.int32, (r, LANES), 1) // SSM_PAIR
    for c in range(ncol):
        steps = [h_scr[c, pl.ds(k, r, stride=SSM_CHUNK), :] for k in range(SSM_CHUNK)]
        for pp in range(QUAD):
            cols = []
            for kc in range(SSM_CHUNK // QUAD):
                acc = None
                for jq in range(QUAD):
                    v = steps[QUAD * kc + jq]
                    shift = ((jq - pp) * SSM_PAIR) % LANES
                    if shift:
                        v = pltpu.roll(v, shift, 1)
                    acc = v if acc is None else jnp.where(lane_q == jq, v, acc)
                cols.append(acc)
            u_ref[QUAD * c + pp] = jnp.concatenate(cols, axis=1).astype(BF16)


def _prenorm(x, g, mod, bc, n, b):
    nt = x.shape[0]
    tm = ROW_TILE
    r = tm // SSM_CHUNK
    row = pl.BlockSpec((tm, D_MODEL), lambda t: (t, 0))
    return pl.pallas_call(
        _prenorm_kernel,
        grid=(nt // tm,),
        in_specs=[row, pl.BlockSpec((1, D_MODEL), lambda t: (0, 0)),
                  _mod_spec(0, tm, bc, n, b), _mod_spec(1, tm, bc, n, b)],
        out_specs=pl.BlockSpec((N_PAIRS, r, PAIR_W), lambda t: (0, t, 0)),
        out_shape=jax.ShapeDtypeStruct((N_PAIRS, nt // SSM_CHUNK, PAIR_W), BF16),
        scratch_shapes=[pltpu.VMEM((D_MODEL // LANES, tm, LANES), F32)],
        compiler_params=_cparams(("parallel",)),
        name="s5_prenorm",
    )(x, g, mod, mod)


def _cmul_add(xr, xi, lr, li, tr, ti):
    return xr + lr * tr - li * ti, xi + lr * ti + li * tr


def _block_scan(er, ei, tab_r, tab_i, s_r, s_i, row, reverse):
    xr, xi = er, ei
    for k in (1, 2, 4):
        if reverse:
            keep = row < SCAN_ROWS - k
            lr, li = tab_r[SCAN_ROWS - k:SCAN_ROWS - k + 1], tab_i[SCAN_ROWS - k:SCAN_ROWS - k + 1]
            shift = SCAN_ROWS - k
        else:
            keep = row >= k
            lr, li = tab_r[k - 1:k], tab_i[k - 1:k]
            shift = k
        tr = jnp.where(keep, pltpu.roll(xr, shift, 0), 0.0)
        ti = jnp.where(keep, pltpu.roll(xi, shift, 0), 0.0)
        xr, xi = _cmul_add(xr, xi, lr, li, tr, ti)
    ar, ai = _cmul_add(xr, xi, tab_r, tab_i, s_r, s_i)
    if reverse:
        edge = row == SCAN_ROWS - 1
        inr = jnp.where(edge, s_r, pltpu.roll(ar, SCAN_ROWS - 1, 0))
        ini = jnp.where(edge, s_i, pltpu.roll(ai, SCAN_ROWS - 1, 0))
        return inr, ini, ar[0:1], ai[0:1]
    edge = row == 0
    inr = jnp.where(edge, s_r, pltpu.roll(ar, 1, 0))
    ini = jnp.where(edge, s_i, pltpu.roll(ai, 1, 0))
    return inr, ini, ar[SCAN_ROWS - 1:SCAN_ROWS], ai[SCAN_ROWS - 1:SCAN_ROWS]


def _s5_core_kernel(u_ref, w3_ref, w2t_ref, quad_ref, tab_ref, y_ref, e_scr, s_scr, tz_scr, *,
                    b, n_ctx_chunks, n_chunks):
    for k in range(SSM_CHUNK):
        for q in range(SSM_CHUNK // QUAD):
            tz_scr[k * SSM_PAIR:(k + 1) * SSM_PAIR, q * LANES:(q + 1) * LANES] = (
                quad_ref[QUAD * q - k + SSM_CHUNK - 1])
    u = u_ref[...]
    e_scr[...] = jnp.dot(u, w3_ref[...], preferred_element_type=F32)
    q4 = LANES
    tfr, tfi, trr, tri = tab_ref[0], tab_ref[1], tab_ref[2], tab_ref[3]
    row = lax.broadcasted_iota(jnp.int32, (SCAN_ROWS, q4), 0)
    nb_ctx = n_ctx_chunks // SCAN_ROWS
    nb = n_chunks // SCAN_ROWS

    def body(m, carry):
        mr = jnp.where(m < nb_ctx, nb_ctx - 1 - m, nb - 1 - (m - nb_ctx))
        n_lat_chunks = n_chunks - n_ctx_chunks

        def block_rows(bi, blk):
            start = jnp.where(blk < nb_ctx, bi * n_ctx_chunks + blk * SCAN_ROWS,
                              b * n_ctx_chunks + bi * n_lat_chunks + (blk - nb_ctx) * SCAN_ROWS)
            return pl.ds(pl.multiple_of(start, SCAN_ROWS), SCAN_ROWS)

        out = []
        for bi in range(b):
            sfr, sfi, srr, sri = carry[4 * bi:4 * bi + 4]
            rf = block_rows(bi, m)
            rr = block_rows(bi, mr)
            inr, ini, sfr, sfi = _block_scan(e_scr[rf, 0:q4], e_scr[rf, q4:2 * q4], tfr, tfi, sfr, sfi, row, False)
            s_scr[rf, 0:q4] = inr
            s_scr[rf, q4:2 * q4] = ini
            inr, ini, srr, sri = _block_scan(e_scr[rr, 2 * q4:3 * q4], e_scr[rr, 3 * q4:4 * q4], trr, tri,
                                             srr, sri, row, True)
            s_scr[rr, 2 * q4:3 * q4] = inr
            s_scr[rr, 3 * q4:4 * q4] = ini
            out += [sfr, sfi, srr, sri]
        return tuple(out)

    z = jnp.zeros((1, q4), F32)
    lax.fori_loop(0, nb, body, (z,) * (4 * b))
    y = jnp.dot(u, tz_scr[...], preferred_element_type=F32)
    y = y + lax.dot_general(s_scr[...].astype(BF16), w2t_ref[...], (((1,), (1,)), ((), ())),
                            preferred_element_type=F32)
    y_ref[...] = y.astype(BF16)


def _s5_core(u, w3, w2t, quads, tab, b, n_ctx_chunks, n_chunks):
    rows = u.shape[1]
    mat = pl.BlockSpec((None, PAIR_W, PAIR_W), lambda p: (p, 0, 0))
    seq = pl.BlockSpec((None, rows, PAIR_W), lambda p: (p, 0, 0))
    return pl.pallas_call(
        functools.partial(_s5_core_kernel, b=b, n_ctx_chunks=n_ctx_chunks, n_chunks=n_chunks),
        grid=(N_PAIRS,),
        in_specs=[seq, mat, mat,
                  pl.BlockSpec((None, N_QUADS, SSM_PAIR, LANES), lambda p: (p, 0, 0, 0)),
                  pl.BlockSpec((None, 4, SCAN_ROWS, LANES), lambda p: (p, 0, 0, 0))],
        out_specs=seq,
        out_shape=jax.ShapeDtypeStruct((N_PAIRS, rows, PAIR_W), BF16),
        scratch_shapes=[pltpu.VMEM((rows, PAIR_W), F32), pltpu.VMEM((rows, PAIR_W), F32),
                        pltpu.VMEM((PAIR_W, PAIR_W), BF16)],
        compiler_params=_cparams(("parallel",)),
        name="s5_core",
    )(u, w3, w2t, quads, tab)


def _s5_out_kernel(x_ref, yp_ref, g_ref, sh_ref, sc_ref, gt_ref, d_ref, w_ref, wb_ref, o_ref, y_scr):
    x = x_ref[...]
    tm = x.shape[0]
    r = tm // SSM_CHUNK
    lane_q = lax.broadcasted_iota(jnp.int32, (r, LANES), 1) // SSM_PAIR
    ncol = D_MODEL // LANES
    for c in range(ncol):
        for kc in range(SSM_CHUNK // QUAD):
            src = [yp_ref[QUAD * c + pp, :, kc * LANES:(kc + 1) * LANES].astype(F32) for pp in range(QUAD)]
            for jq in range(QUAD):
                acc = None
                for pp in range(QUAD):
                    v = src[pp]
                    shift = ((pp - jq) * SSM_PAIR) % LANES
                    if shift:
                        v = pltpu.roll(v, shift, 1)
                    acc = v if acc is None else jnp.where(lane_q == pp, v, acc)
                y_scr[c, pl.ds(QUAD * kc + jq, r, stride=SSM_CHUNK), :] = acc
    y = jnp.concatenate([y_scr[c] for c in range(ncol)], axis=1)
    h = _norm_mod(x, g_ref[...], sh_ref[...], sc_ref[...])
    yy = y + d_ref[...] * h
    z = jax.nn.gelu(yy).astype(BF16)
    ag = jnp.dot(z, w_ref[...], preferred_element_type=F32) + wb_ref[...]
    a = ag[:, :D_MODEL]
    g = ag[:, D_MODEL:]
    o_ref[...] = x + gt_ref[...] * (a * jax.nn.sigmoid(g))


def _s5_out(x, yp, g, mod, d, glu_w, glu_b, bc, n, b):
    nt = x.shape[0]
    tm = ROW_TILE
    row = pl.BlockSpec((tm, D_MODEL), lambda t: (t, 0))
    vec = pl.BlockSpec((1, D_MODEL), lambda t: (0, 0))
    return pl.pallas_call(
        _s5_out_kernel,
        grid=(nt // tm,),
        in_specs=[row, pl.BlockSpec((N_PAIRS, tm // SSM_CHUNK, PAIR_W), lambda t: (0, t, 0)), vec,
                  _mod_spec(0, tm, bc, n, b), _mod_spec(1, tm, bc, n, b), _mod_spec(2, tm, bc, n, b),
                  vec,
                  pl.BlockSpec((D_MODEL, 2 * D_MODEL), lambda t: (0, 0)),
                  pl.BlockSpec((1, 2 * D_MODEL), lambda t: (0, 0))],
        out_specs=row,
        out_shape=jax.ShapeDtypeStruct((nt, D_MODEL), F32),
        scratch_shapes=[pltpu.VMEM((D_MODEL // LANES, tm, LANES), F32)],
        compiler_params=_cparams(("parallel",)),
        name="s5_out",
    )(x, yp, g, mod, mod, mod, d, glu_w, glu_b)


def _pair_lanes(x):
    z = jnp.zeros_like(x[..., 0, :, :])
    top = jnp.concatenate([x[..., 0, :, :], z], axis=-1)
    bot = jnp.concatenate([z, x[..., 1, :, :]], axis=-1)
    return jnp.stack([top, bot], axis=-3)


def _s5_chunk_weights(a_re, a_im, log_dt, b_re, b_im, c_re, c_im):
    L = SSM_CHUNK
    G, P, I = SSM_GROUPS, SSM_STATE, SSM_GROUP
    a_re = a_re.astype(F32)
    a_im = a_im.astype(F32)
    dt = jnp.exp(log_dt.astype(F32))[..., None]
    mag = jnp.exp(a_re * dt)
    lam_re = mag * jnp.cos(a_im * dt)
    lam_im = mag * jnp.sin(a_im * dt)
    den = a_re * a_re + a_im * a_im
    num_re = lam_re - 1.0
    f_re = (num_re * a_re + lam_im * a_im) / den
    f_im = (lam_im * a_re - num_re * a_im) / den
    b_re = b_re.astype(F32)
    b_im = b_im.astype(F32)
    bb_re = f_re[..., None] * b_re - f_im[..., None] * b_im
    bb_im = f_re[..., None] * b_im + f_im[..., None] * b_re
    m = jnp.arange(L + 1, dtype=F32)[:, None, None, None]
    pmag = jnp.exp(m * (a_re * dt)[None])
    pw_re = pmag * jnp.cos(m * (a_im * dt)[None])
    pw_im = pmag * jnp.sin(m * (a_im * dt)[None])
    c_re = c_re.astype(F32)
    c_im = c_im.astype(F32)
    cl_re = c_re[None] * pw_re[:, :, :, None, :] - c_im[None] * pw_im[:, :, :, None, :]
    cl_im = c_re[None] * pw_im[:, :, :, None, :] + c_im[None] * pw_re[:, :, :, None, :]
    bt_re = jnp.swapaxes(bb_re, -1, -2)
    bt_im = jnp.swapaxes(bb_im, -1, -2)
    lb_re = pw_re[:, :, :, None, :] * bt_re[None] - pw_im[:, :, :, None, :] * bt_im[None]
    lb_im = pw_re[:, :, :, None, :] * bt_im[None] + pw_im[:, :, :, None, :] * bt_re[None]

    def pair_rows(parts):
        cols = []
        for x in parts:
            x = x.reshape(L, N_PAIRS, 2, x.shape[-2], P)
            cols.append(_pair_lanes(jnp.transpose(x, (1, 0, 2, 3, 4))))
        return jnp.concatenate(cols, axis=-1).reshape(N_PAIRS, PAIR_W, 4 * 2 * P)

    pf = (L - 1) - jnp.arange(L)
    pr = jnp.arange(L)
    w3 = pair_rows([lb_re[pf, 0], lb_im[pf, 0], lb_re[pr, 1], lb_im[pr, 1]])
    qf = jnp.arange(L) + 1
    qr = L - jnp.arange(L)
    w2t = pair_rows([cl_re[qf, 0], -cl_im[qf, 0], cl_re[qr, 1], -cl_im[qr, 1]])
    kk = (jnp.sum(cl_re[:L, :, :, None, :, :] * bt_re[None, :, :, :, None, :], axis=-1)
          - jnp.sum(cl_im[:L, :, :, None, :, :] * bt_im[None, :, :, :, None, :], axis=-1))
    kp = _pair_lanes(kk.reshape(L, 2, N_PAIRS, 2, I, I)).reshape(L, 2, N_PAIRS, SSM_PAIR, SSM_PAIR)
    lags = jnp.concatenate([kp[:0:-1, 1], kp[:1, 0] + kp[:1, 1], kp[1:, 0]], axis=0)
    quads = jnp.concatenate([lags[q:q + N_QUADS] for q in range(QUAD)], axis=-1)
    quads = jnp.transpose(quads, (1, 0, 2, 3))
    jf = (jnp.arange(SCAN_ROWS, dtype=F32) + 1.0) * L
    jr = (SCAN_ROWS - jnp.arange(SCAN_ROWS, dtype=F32)) * L
    jj = jnp.stack([jf, jr], axis=0)[:, :, None, None]
    tmag = jnp.exp(jj * (a_re * dt)[:, None])
    t_re = tmag * jnp.cos(jj * (a_im * dt)[:, None])
    t_im = tmag * jnp.sin(jj * (a_im * dt)[:, None])
    tab = jnp.stack([t_re[0], t_im[0], t_re[1], t_im[1]], axis=0)
    tab = jnp.transpose(tab.reshape(4, SCAN_ROWS, N_PAIRS, 2 * P), (2, 0, 1, 3))
    return w3.astype(BF16), w2t.astype(BF16), quads.astype(BF16), tab


def _s5_layer(x, g, mod, ssm, bc, n, b):
    (a_re, a_im, log_dt, b_re, b_im, c_re, c_im, d, glu_w, glu_b) = ssm
    c_len = bc // b
    n_ctx_chunks = c_len // SSM_CHUNK
    n_chunks = (c_len + n) // SSM_CHUNK
    u = _prenorm(x, g, mod, bc, n, b)
    w3, w2t, quads, tab = _s5_chunk_weights(a_re, a_im, log_dt, b_re, b_im, c_re, c_im)
    yp = _s5_core(u, w3, w2t, quads, tab, b, n_ctx_chunks, n_chunks)
    return _s5_out(x, yp, g, mod, d.reshape(1, D_MODEL), glu_w.astype(BF16),
                   glu_b.reshape(1, 2 * D_MODEL), bc, n, b)


def _ffn_kernel(x_ref, g_ref, sh_ref, sc_ref, gt_ref, wg_ref, wu_ref, wd_ref, o_ref, h_scr, acc_scr):
    j = pl.program_id(1)

    @pl.when(j == 0)
    def _():
        h_scr[...] = _norm_mod(x_ref[...], g_ref[...], sh_ref[...], sc_ref[...]).astype(BF16)
        acc_scr[...] = jnp.zeros_like(acc_scr)

    h = h_scr[...]
    gate = jnp.dot(h, wg_ref[...], preferred_element_type=F32)
    up = jnp.dot(h, wu_ref[...], preferred_element_type=F32)
    a = (gate * jax.nn.sigmoid(gate) * up).astype(BF16)
    acc_scr[...] += jnp.dot(a, wd_ref[...], preferred_element_type=F32)

    @pl.when(j == pl.num_programs(1) - 1)
    def _():
        o_ref[...] = x_ref[...] + gt_ref[...] * acc_scr[...]


def _ffn(x, g, mod, w_gate_up, w_down, bc, n, b):
    nt = x.shape[0]
    tm = ROW_TILE
    nj = D_FF // FF_TILE
    row = pl.BlockSpec((tm, D_MODEL), lambda t, j: (t, 0))
    return pl.pallas_call(
        _ffn_kernel,
        grid=(nt // tm, nj),
        in_specs=[row, pl.BlockSpec((1, D_MODEL), lambda t, j: (0, 0)),
                  _mod_spec(3, tm, bc, n, b), _mod_spec(4, tm, bc, n, b), _mod_spec(5, tm, bc, n, b),
                  pl.BlockSpec((D_MODEL, FF_TILE), lambda t, j: (0, j)),
                  pl.BlockSpec((D_MODEL, FF_TILE), lambda t, j: (0, nj + j)),
                  pl.BlockSpec((FF_TILE, D_MODEL), lambda t, j: (j, 0))],
        out_specs=row,
        out_shape=jax.ShapeDtypeStruct((nt, D_MODEL), F32),
        scratch_shapes=[pltpu.VMEM((tm, D_MODEL), BF16), pltpu.VMEM((tm, D_MODEL), F32)],
        compiler_params=_cparams(("parallel", "arbitrary")),
        name="ffn_dense",
    )(x, g, mod, mod, mod, w_gate_up, w_gate_up, w_down)


def _head_ms(sq, ones):
    return jnp.dot(sq.astype(BF16), ones, preferred_element_type=F32)


def _rope(v, cos, sin_signed, first_half):
    w = v.shape[-1]
    quarter = HEAD_DIM // 4
    swapped = jnp.where(first_half, pltpu.roll(v, w - quarter, 1), pltpu.roll(v, quarter, 1))
    return v * cos + swapped * sin_signed


def _qkv_kernel(x_ref, g_ref, sh_ref, sc_ref, w_ref, qg_ref, kg_ref, ones_ref, cos_ref, sin_ref,
                q_ref, k_ref, v_ref):
    tm = x_ref.shape[0]
    h = _norm_mod(x_ref[...], g_ref[...], sh_ref[...], sc_ref[...]).astype(BF16)
    qkv = jnp.dot(h, w_ref[...], preferred_element_type=F32)
    dq = N_Q_HEADS * HEAD_DIM
    dk = N_KV_HEADS * HEAD_DIM
    q = qkv[:, :dq]
    k = qkv[:, dq:dq + dk]
    v = qkv[:, dq + dk:]
    ones = ones_ref[...]
    q = q * lax.rsqrt(_head_ms(q * q, ones) + EPS) * qg_ref[...]
    k = k * lax.rsqrt(_head_ms(k * k, ones[:dk, :dk]) + EPS) * kg_ref[...]
    cos = cos_ref[...]
    sin = sin_ref[...]
    lane_q = lax.broadcasted_iota(jnp.int32, (tm, dq), 1)
    lane_k = lax.broadcasted_iota(jnp.int32, (tm, dk), 1)
    half = HEAD_DIM // 2
    quarter = HEAD_DIM // 4
    q = _rope(q, jnp.tile(cos, (1, dq // LANES)), jnp.tile(sin, (1, dq // LANES)), (lane_q % half) < quarter)
    k = _rope(k, jnp.tile(cos, (1, dk // LANES)), jnp.tile(sin, (1, dk // LANES)), (lane_k % half) < quarter)
    q = q * (HEAD_DIM ** -0.5)
    k_ref[...] = k.astype(BF16)
    v_ref[...] = v.astype(BF16)
    lane = lax.broadcasted_iota(jnp.int32, (tm, LANES), 1)
    low = lane < HEAD_DIM
    for c in range(dq // LANES):
        vc = q[:, c * LANES:(c + 1) * LANES]
        vr = pltpu.roll(vc, HEAD_DIM, 1)
        kv_half = (c // 2) % 2
        if kv_half == 0:
            first = jnp.where(low, vc, 0.0)
            second = jnp.where(low, vr, 0.0)
        else:
            first = jnp.where(low, 0.0, vr)
            second = jnp.where(low, 0.0, vc)
        q_ref[:, (2 * c) * LANES:(2 * c + 1) * LANES] = first.astype(BF16)
        q_ref[:, (2 * c + 1) * LANES:(2 * c + 2) * LANES] = second.astype(BF16)


def _qkv(x, g, mod, w_qkv, q_g, k_g, ones, cos, sin, bc, n, b):
    nt = x.shape[0]
    tm = ROW_TILE
    dq = N_Q_HEADS * HEAD_DIM
    dk = N_KV_HEADS * HEAD_DIM
    row = lambda w: pl.BlockSpec((tm, w), lambda t: (t, 0))
    full = lambda r, w: pl.BlockSpec((r, w), lambda t: (0, 0))
    return pl.pallas_call(
        _qkv_kernel,
        grid=(nt // tm,),
        in_specs=[row(D_MODEL), full(1, D_MODEL), _mod_spec(0, tm, bc, n, b), _mod_spec(1, tm, bc, n, b),
                  full(D_MODEL, dq + 2 * dk), full(1, dq), full(1, dk), full(dq, dq),
                  row(LANES), row(LANES)],
        out_specs=[row(2 * dq), row(dk), row(dk)],
        out_shape=[jax.ShapeDtypeStruct((nt, 2 * dq), BF16),
                   jax.ShapeDtypeStruct((nt, dk), BF16),
                   jax.ShapeDtypeStruct((nt, dk), BF16)],
        compiler_params=_cparams(("parallel",)),
        name="attn_qkv",
    )(x, g, mod, mod, w_qkv, q_g, k_g, ones, cos, sin)


def _attn_kernel(*refs, sub, n_sub, n_lat):
    if n_lat:
        q_ref, kc_ref, vc_ref, kl_ref, vl_ref, o_ref = refs[:6]
    else:
        q_ref, kc_ref, vc_ref, o_ref = refs[:4]
    s_bufs = refs[-2 * ATTN_DEPTH:-ATTN_DEPTH]
    m_bufs = refs[-ATTN_DEPTH:]
    grp = pl.program_id(1)
    nt_dims = (((1,), (1,)), ((), ()))
    c_len = kc_ref.shape[0]
    parts = [(kc_ref, vc_ref, 0, 0, c_len)]
    if n_lat:
        kc = min(ATTN_KEY_CHUNK, n_lat)
        parts += [(kl_ref, vl_ref, c0, c_len + c0, kc) for c0 in range(0, n_lat, kc)]
    lane = lax.broadcasted_iota(jnp.int32, (sub, LANES), 1)
    low = lane < HEAD_DIM
    upper_half = (grp % 2) == 1

    def scores(i, s_scr, m_scr):
        r0 = pl.multiple_of(i * sub, sub)
        qs = jnp.concatenate([q_ref[pl.ds(r0, sub), a * LANES:(a + 1) * LANES] for a in range(Q_PER_KV)],
                             axis=0)
        m = None
        for kr, _, k0, off, sz in parts:
            s = lax.dot_general(qs, kr[k0:k0 + sz, :], nt_dims, preferred_element_type=F32)
            s_scr[:, off:off + sz] = s
            mx = s.max(axis=-1, keepdims=True)
            m = mx if m is None else jnp.maximum(m, mx)
        m_scr[...] = m

    def output(i, s_scr, m_scr):
        r0 = pl.multiple_of(i * sub, sub)
        m = m_scr[...]
        l = jnp.zeros_like(m)
        acc = jnp.zeros((Q_PER_KV * sub, LANES), F32)
        for _, vr, k0, off, sz in parts:
            p = jnp.exp(s_scr[:, off:off + sz] - m)
            l = l + p.sum(axis=-1, keepdims=True)
            acc = acc + jnp.dot(p.astype(BF16), vr[k0:k0 + sz, :], preferred_element_type=F32)
        o = acc / l
        for c in range(Q_PER_KV // 2):
            a0 = o[(2 * c) * sub:(2 * c + 1) * sub]
            a1 = o[(2 * c + 1) * sub:(2 * c + 2) * sub]
            from_low = jnp.where(low, a0, pltpu.roll(a1, HEAD_DIM, 1))
            from_high = jnp.where(low, pltpu.roll(a0, HEAD_DIM, 1), a1)
            o_ref[pl.ds(r0, sub), c * LANES:(c + 1) * LANES] = (
                jnp.where(upper_half, from_high, from_low).astype(BF16))

    depth = ATTN_DEPTH
    ahead = depth - 1

    def stage(i, r):
        output(i, s_bufs[r], m_bufs[r])
        nxt = (r + ahead) % depth
        scores(i + ahead, s_bufs[nxt], m_bufs[nxt])

    for t in range(min(ahead, n_sub)):
        scores(t, s_bufs[t], m_bufs[t])
    n_loop = max(n_sub - ahead, 0) // depth

    def body(j, carry):
        for r in range(depth):
            stage(j * depth + r, r)
        return carry

    if n_loop:
        lax.fori_loop(0, n_loop, body, 0)
    for i in range(n_loop * depth, n_sub):
        if i + ahead < n_sub:
            stage(i, i % depth)
        else:
            output(i, s_bufs[i % depth], m_bufs[i % depth])


def _attn_call(name, tq, n_lat, c_len, grid, in_specs, out_spec, out_rows, args):
    rows = Q_PER_KV * ATTN_SUB
    assert tq % ATTN_SUB == 0
    return pl.pallas_call(
        functools.partial(_attn_kernel, sub=ATTN_SUB, n_sub=tq // ATTN_SUB, n_lat=n_lat),
        grid=grid,
        in_specs=in_specs,
        out_specs=out_spec,
        out_shape=jax.ShapeDtypeStruct((out_rows, D_MODEL), BF16),
        scratch_shapes=([pltpu.VMEM((rows, c_len + n_lat), F32)] * ATTN_DEPTH
                        + [pltpu.VMEM((rows, 1), F32)] * ATTN_DEPTH),
        compiler_params=_cparams(("parallel", "parallel", "parallel")),
        name=name,
    )(*args)


def _attn_latent(qz, k, v, bc, n, b):
    c_len = bc // b
    tq = min(Q_TILE, n)
    assert bc % tq == 0 and n % tq == 0
    gw = Q_PER_KV * LANES
    k_lat, v_lat = k[bc:], v[bc:]
    q_spec = pl.BlockSpec((tq, gw), lambda bi, g, qi: ((bc + bi * n) // tq + qi, g))
    ctx_spec = pl.BlockSpec((c_len, LANES), lambda bi, g, qi: (bi, g // 2))
    lat_spec = pl.BlockSpec((n, LANES), lambda bi, g, qi: (bi, g // 2))
    o_spec = pl.BlockSpec((tq, Q_PER_KV * HEAD_DIM), lambda bi, g, qi: (bi * (n // tq) + qi, g))
    return _attn_call("attn_latent", tq, n, c_len, (b, N_KV_HEADS, n // tq),
                      [q_spec, ctx_spec, ctx_spec, lat_spec, lat_spec], o_spec, b * n,
                      (qz, k, v, k_lat, v_lat))


def _attn_ctx(qz, k, v, bc, b):
    c_len = bc // b
    gw = Q_PER_KV * LANES
    q_spec = pl.BlockSpec((c_len, gw), lambda bi, g, qi: (bi, g))
    ctx_spec = pl.BlockSpec((c_len, LANES), lambda bi, g, qi: (bi, g // 2))
    o_spec = pl.BlockSpec((c_len, Q_PER_KV * HEAD_DIM), lambda bi, g, qi: (bi, g))
    return _attn_call("attn_ctx", c_len, 0, c_len, (b, N_KV_HEADS, 1),
                      [q_spec, ctx_spec, ctx_spec], o_spec, bc, (qz, k, v))


def _attn_out_kernel(o_ref, x_ref, wo_ref, gt_ref, g_ref, sh_ref, sc_ref, rw_ref, rb_ref,
                     xo_ref, h_ref, r_ref):
    tm = x_ref.shape[0]
    x = x_ref[...] + gt_ref[...] * jnp.dot(o_ref[...], wo_ref[...], preferred_element_type=F32)
    xo_ref[...] = x
    h = _norm_mod(x, g_ref[...], sh_ref[...], sc_ref[...])
    h_ref[...] = h
    logits = jnp.dot(h, rw_ref[...], preferred_element_type=F32, precision=lax.Precision.HIGHEST) + rb_ref[...]
    lane = lax.broadcasted_iota(jnp.int32, (tm, LANES), 1)
    neg = -jnp.inf
    logits = jnp.where(lane < N_EXPERTS, logits, neg)
    m1 = jnp.max(logits, axis=-1, keepdims=True)
    i1 = jnp.min(jnp.where(logits == m1, lane, LANES), axis=-1, keepdims=True)
    rest = jnp.where(lane == i1, neg, logits)
    m2 = jnp.max(rest, axis=-1, keepdims=True)
    i2 = jnp.min(jnp.where(rest == m2, lane, LANES), axis=-1, keepdims=True)
    e = jnp.exp(m2 - m1)
    w1 = 1.0 / (1.0 + e)
    w2 = e / (1.0 + e)
    r = jnp.where(lane == 0, i1.astype(F32), 0.0)
    r = jnp.where(lane == 1, i2.astype(F32), r)
    r = jnp.where(lane == 2, w1, r)
    r = jnp.where(lane == 3, w2, r)
    r_ref[...] = r


def _attn_out(o, x, w_o, g, mod, router_w, router_b, bc, n, b):
    nt = x.shape[0]
    tm = ROW_TILE
    row = lambda w: pl.BlockSpec((tm, w), lambda t: (t, 0))
    full = lambda r, w: pl.BlockSpec((r, w), lambda t: (0, 0))
    return pl.pallas_call(
        _attn_out_kernel,
        grid=(nt // tm,),
        in_specs=[row(D_MODEL), row(D_MODEL), full(D_MODEL, D_MODEL), _mod_spec(2, tm, bc, n, b),
                  full(1, D_MODEL), _mod_spec(3, tm, bc, n, b), _mod_spec(4, tm, bc, n, b),
                  full(D_MODEL, LANES), full(1, LANES)],
        out_specs=[row(D_MODEL), row(D_MODEL), row(LANES)],
        out_shape=[jax.ShapeDtypeStruct((nt, D_MODEL), F32),
                   jax.ShapeDtypeStruct((nt, D_MODEL), F32),
                   jax.ShapeDtypeStruct((nt, LANES), F32)],
        compiler_params=_cparams(("parallel",)),
        name="attn_out_router",
    )(o, x, w_o, mod, g, mod, mod, router_w, router_b)


def _moe_ffn_kernel(te_ref, tv_ref, x_ref, wg_ref, wu_ref, wd_ref, o_ref, h_scr, acc_scr):
    i = pl.program_id(0)
    j = pl.program_id(1)
    valid = tv_ref[i] > 0

    @pl.when(j == 0)
    def _():
        h_scr[...] = x_ref[...].astype(BF16)
        acc_scr[...] = jnp.zeros_like(acc_scr)

    @pl.when(valid)
    def _():
        h = h_scr[...]
        gate = jnp.dot(h, wg_ref[...], preferred_element_type=F32)
        up = jnp.dot(h, wu_ref[...], preferred_element_type=F32)
        a = (gate * jax.nn.sigmoid(gate) * up).astype(BF16)
        acc_scr[...] += jnp.dot(a, wd_ref[...], preferred_element_type=F32)

    @pl.when(j == pl.num_programs(1) - 1)
    def _():
        o_ref[...] = acc_scr[...].astype(BF16)


def _moe_ffn(xs, tile_expert, tile_valid, w_gate_up, w_down):
    slots = xs.shape[0]
    tm = MOE_TILE
    nj = D_FF // FF_TILE
    row = pl.BlockSpec((tm, D_MODEL), lambda i, j, te, tv: (i, 0))
    grid_spec = pltpu.PrefetchScalarGridSpec(
        num_scalar_prefetch=2,
        grid=(slots // tm, nj),
        in_specs=[row,
                  pl.BlockSpec((None, D_MODEL, FF_TILE), lambda i, j, te, tv: (te[i], 0, j)),
                  pl.BlockSpec((None, D_MODEL, FF_TILE), lambda i, j, te, tv: (te[i], 0, nj + j)),
                  pl.BlockSpec((None, FF_TILE, D_MODEL), lambda i, j, te, tv: (te[i], j, 0))],
        out_specs=row,
        scratch_shapes=[pltpu.VMEM((tm, D_MODEL), BF16), pltpu.VMEM((tm, D_MODEL), F32)])
    return pl.pallas_call(
        _moe_ffn_kernel,
        grid_spec=grid_spec,
        out_shape=jax.ShapeDtypeStruct((slots, D_MODEL), BF16),
        compiler_params=_cparams(("parallel", "arbitrary")),
        name="moe_ffn",
    )(tile_expert, tile_valid, xs, w_gate_up, w_gate_up, w_down)


def _moe_combine_kernel(x_ref, a_ref, b_ref, r_ref, gt_ref, o_ref):
    tm = x_ref.shape[0]
    lane = lax.broadcasted_iota(jnp.int32, (tm, LANES), 1)
    r = r_ref[...]
    w1 = jnp.sum(jnp.where(lane == 2, r, 0.0), axis=-1, keepdims=True)
    w2 = jnp.sum(jnp.where(lane == 3, r, 0.0), axis=-1, keepdims=True)
    y = w1 * a_ref[...].astype(F32) + w2 * b_ref[...].astype(F32)
    o_ref[...] = x_ref[...] + gt_ref[...] * y


def _moe_combine(x, ya, yb, route, mod, bc, n, b):
    nt = x.shape[0]
    tm = ROW_TILE
    row = lambda w: pl.BlockSpec((tm, w), lambda t: (t, 0))
    return pl.pallas_call(
        _moe_combine_kernel,
        grid=(nt // tm,),
        in_specs=[row(D_MODEL), row(D_MODEL), row(D_MODEL), row(LANES), _mod_spec(5, tm, bc, n, b)],
        out_specs=row(D_MODEL),
        out_shape=jax.ShapeDtypeStruct((nt, D_MODEL), F32),
        compiler_params=_cparams(("parallel",)),
        name="moe_combine",
    )(x, ya, yb, route, mod)


def _moe_layer(x, h, route, mod, w_gate_up, w_down, bc, n, b):
    nt = x.shape[0]
    tm = MOE_TILE
    top_i = route[:, :TOP_K].astype(jnp.int32)
    flat_e = top_i.reshape(-1)
    onehot = (flat_e[:, None] == jnp.arange(N_EXPERTS)[None, :]).astype(jnp.int32)
    counts = jnp.sum(onehot, axis=0)
    rank = jnp.sum((jnp.cumsum(onehot, axis=0) - onehot) * onehot, axis=1)
    padded = ((counts + tm - 1) // tm) * tm
    ends = jnp.cumsum(padded)
    starts = ends - padded
    pos = starts[flat_e] + rank
    n_tiles = (TOP_K * nt) // tm + N_EXPERTS
    slots = n_tiles * tm
    token_of_slot = jnp.zeros((slots,), jnp.int32).at[pos].set(jnp.arange(TOP_K * nt, dtype=jnp.int32) // TOP_K)
    tile_start = jnp.arange(n_tiles, dtype=jnp.int32) * tm
    tile_expert = jnp.sum((tile_start[:, None] >= ends[None, :]).astype(jnp.int32), axis=1)
    tile_valid = (tile_start < ends[-1]).astype(jnp.int32)
    last_expert = jnp.max(jnp.where(counts > 0, jnp.arange(N_EXPERTS), 0))
    tile_expert = jnp.where(tile_valid > 0, tile_expert, last_expert).astype(jnp.int32)
    xs = h.at[token_of_slot].get(mode="promise_in_bounds")
    ys = _moe_ffn(xs, tile_expert, tile_valid, w_gate_up, w_down)
    pos2 = pos.reshape(nt, TOP_K)
    ya = ys.at[pos2[:, 0]].get(mode="promise_in_bounds")
    yb = ys.at[pos2[:, 1]].get(mode="promise_in_bounds")
    return _moe_combine(x, ya, yb, route, mod, bc, n, b)


def _rope_tables(bc, n):
    rows = n // GRID_W
    row = jnp.broadcast_to(jnp.arange(rows)[:, None], (rows, GRID_W)).reshape(-1).astype(F32)
    col = jnp.broadcast_to(jnp.arange(GRID_W)[None, :], (rows, GRID_W)).reshape(-1).astype(F32)
    axis_dim = HEAD_DIM // 2
    inv = ROPE_THETA ** (-jnp.arange(0, axis_dim, 2, dtype=F32) / axis_dim)
    ar = row[:, None] * inv
    ac = col[:, None] * inv
    cos = jnp.concatenate([jnp.cos(ar), jnp.cos(ar), jnp.cos(ac), jnp.cos(ac)], axis=1)
    sin = jnp.concatenate([-jnp.sin(ar), jnp.sin(ar), -jnp.sin(ac), jnp.sin(ac)], axis=1)
    cos = jnp.tile(cos, (1, LANES // HEAD_DIM))
    sin = jnp.tile(sin, (1, LANES // HEAD_DIM))
    return cos, sin


def kernel(x, c, ctx, c_ctx, ada_w, ada_b, norm_mix_g, norm_ffn_g, ssm_a_re, ssm_a_im, ssm_log_dt,
           ssm_b_re, ssm_b_im, ssm_c_re, ssm_c_im, ssm_d, ssm_glu_w, ssm_glu_b, attn_w_qkv, attn_q_g,
           attn_k_g, attn_w_o, ffn_w_gate_up, ffn_w_down, moe_router_w, moe_router_b, moe_w_gate_up,
           moe_w_down):
    b, n, _ = x.shape
    c_len = ctx.shape[1]
    bc = b * c_len
    assert b + 1 <= MOD_ROWS and bc % ROW_TILE == 0 and n % ROW_TILE == 0
    assert c_len % SSM_CHUNK == 0 and n % SSM_CHUNK == 0 and n % GRID_W == 0

    cvecs = jnp.zeros((MOD_ROWS, D_MODEL), F32).at[:b].set(c).at[b].set(c_ctx)
    mods = _modulation_table(cvecs, ada_w, ada_b).reshape(DEPTH, MOD_ROWS, 1, N_MOD * D_MODEL)

    xs = jnp.concatenate([ctx.reshape(bc, D_MODEL), x.reshape(b * n, D_MODEL)], axis=0)

    cos_l, sin_l = _rope_tables(bc, n)
    cos = jnp.concatenate([jnp.ones((bc, LANES), F32), jnp.tile(cos_l, (b, 1))], axis=0)
    sin = jnp.concatenate([jnp.zeros((bc, LANES), F32), jnp.tile(sin_l, (b, 1))], axis=0)
    dq = N_Q_HEADS * HEAD_DIM
    hid = jnp.arange(dq) // HEAD_DIM
    ones = ((hid[:, None] == hid[None, :]).astype(F32) / HEAD_DIM).astype(BF16)

    for i in range(DEPTH):
        j = i // 2
        last = i == DEPTH - 1
        mod = mods[i]
        g_mix = norm_mix_g[i].reshape(1, D_MODEL)
        g_ffn = norm_ffn_g[i].reshape(1, D_MODEL)
        if i % 2 == 0:
            ssm = (ssm_a_re[j], ssm_a_im[j], ssm_log_dt[j], ssm_b_re[j], ssm_b_im[j], ssm_c_re[j],
                   ssm_c_im[j], ssm_d[j], ssm_glu_w[j], ssm_glu_b[j])
            xs = _s5_layer(xs, g_mix, mod, ssm, bc, n, b)
            xs = _ffn(xs, g_ffn, mod, _to_bf16(ffn_w_gate_up, j), _to_bf16(ffn_w_down, j), bc, n, b)
        else:
            qz, k, v = _qkv(xs, g_mix, mod, attn_w_qkv[j].astype(BF16),
                            jnp.tile(attn_q_g[j], N_Q_HEADS).reshape(1, dq),
                            jnp.tile(attn_k_g[j], N_KV_HEADS).reshape(1, N_KV_HEADS * HEAD_DIM),
                            ones, cos, sin, bc, n, b)
            o = _attn_latent(qz, k, v, bc, n, b)
            if last:
                xs = xs[bc:]
                lay = (0, n, b)
            else:
                o = jnp.concatenate([_attn_ctx(qz, k, v, bc, b), o], axis=0)
                lay = (bc, n, b)
            rw = jnp.zeros((D_MODEL, LANES), F32).at[:, :N_EXPERTS].set(moe_router_w[j])
            rb = jnp.zeros((1, LANES), F32).at[0, :N_EXPERTS].set(moe_router_b[j])
            xs, h, route = _attn_out(o, xs, attn_w_o[j].astype(BF16), g_ffn, mod, rw, rb, *lay)
            xs = _moe_layer(xs, h, route, mod, _to_bf16(moe_w_gate_up, j), _to_bf16(moe_w_down, j), *lay)
    return xs.reshape(b, n, D_MODEL)
```

```python
import functools
import math

import jax
import jax.numpy as jnp
from jax import lax
from jax.experimental import pallas as pl
from jax.experimental.pallas import tpu as pltpu

F32 = jnp.float32
BF16 = jnp.bfloat16

D_MODEL = 1024
DEPTH = 4
GRID_W = 64
SSM_GROUP = 16
SSM_GROUPS = D_MODEL // SSM_GROUP
SSM_STATE = 64
HEAD_DIM = 64
N_Q_HEADS = D_MODEL // HEAD_DIM
N_KV_HEADS = 4
Q_PER_KV = N_Q_HEADS // N_KV_HEADS
ROPE_THETA = 10000.0
D_FF = 2816
N_EXPERTS = 8
TOP_K = 2
N_MOD = 6
EPS = 1e-6

LANES = 128
MOD_ROWS = 8
BF16_ROWS = 16
CAST_BLOCK_BYTES = 8 * 1024 * 1024

ROW_TILE = 512
FF_TILE = 1408
MOE_TILE = 512
SSM_CHUNK = 16
SSM_PAIR = 2 * SSM_GROUP
N_PAIRS = D_MODEL // SSM_PAIR
SCAN_ROWS = 8
QUAD = LANES // SSM_PAIR
N_QUADS = 2 * SSM_CHUNK - QUAD
PAIR_W = SSM_CHUNK * SSM_PAIR
Q_TILE = 1024
ATTN_SUB = 64
ATTN_DEPTH = 3
ATTN_KEY_CHUNK = 1024
VMEM_LIMIT = 56 * 1024 * 1024


def _cparams(sem):
    return pltpu.CompilerParams(dimension_semantics=sem, vmem_limit_bytes=VMEM_LIMIT)


def _row_class(t, tile, bc, n, b):
    r0 = t * tile
    return jnp.where(r0 < bc, b, (r0 - bc) // max(n, 1))


def _mod_spec(piece, tile, bc, n, b):
    return pl.BlockSpec((None, 1, D_MODEL),
                        lambda t, *_: (_row_class(t, tile, bc, n, b), 0, piece))


def _norm_mod(x, g, shift, scale):
    ms = jnp.mean(x * x, axis=-1, keepdims=True)
    return (x * lax.rsqrt(ms + EPS) * g) * (1.0 + scale) + shift


def _cast_kernel(w_ref, o_ref):
    o_ref[...] = w_ref[...].astype(BF16)


def _to_bf16(w, j):
    cols = w.shape[-1]
    w3 = w.reshape(w.shape[0], -1, cols)
    rows = w3.shape[1]
    rb = 1 << ((CAST_BLOCK_BYTES // (4 * cols)).bit_length() - 1)
    while rows % rb:
        rb //= 2
    assert rb % BF16_ROWS == 0
    out = pl.pallas_call(
        _cast_kernel,
        grid=(rows // rb,),
        in_specs=[pl.BlockSpec((None, rb, cols), lambda i: (j, i, 0))],
        out_specs=pl.BlockSpec((rb, cols), lambda i: (i, 0)),
        out_shape=jax.ShapeDtypeStruct((rows, cols), BF16),
        compiler_params=_cparams(("parallel",)),
        name="weights_to_bf16",
    )(w3)
    return out.reshape(w.shape[1:])


def _mod_kernel(c_ref, w_ref, b_ref, o_ref):
    c = c_ref[...]
    s = (c * jax.nn.sigmoid(c)).astype(BF16)
    o_ref[...] = jnp.dot(s, w_ref[...].astype(BF16), preferred_element_type=F32) + b_ref[...]


def _modulation_table(cvecs, ada_w, ada_b):
    tn = 2048
    width = N_MOD * D_MODEL
    return pl.pallas_call(
        _mod_kernel,
        grid=(DEPTH, width // tn),
        in_specs=[pl.BlockSpec((MOD_ROWS, D_MODEL), lambda i, j: (0, 0)),
                  pl.BlockSpec((None, D_MODEL, tn), lambda i, j: (i, 0, j)),
                  pl.BlockSpec((None, 1, tn), lambda i, j: (i, 0, j))],
        out_specs=pl.BlockSpec((None, MOD_ROWS, tn), lambda i, j: (i, 0, j)),
        out_shape=jax.ShapeDtypeStruct((DEPTH, MOD_ROWS, width), F32),
        compiler_params=_cparams(("parallel", "parallel")),
        name="adaln_table",
    )(cvecs, ada_w, ada_b.reshape(DEPTH, 1, width))


def _prenorm_kernel(x_ref, g_ref, sh_ref, sc_ref, u_ref, h_scr):
    tm = x_ref.shape[0]
    r = tm // SSM_CHUNK
    h = _norm_mod(x_ref[...], g_ref[...], sh_ref[...], sc_ref[...])
    ncol = D_MODEL // LANES
    for c in range(ncol):
        h_scr[c] = h[:, c * LANES:(c + 1) * LANES]
    lane_q = lax.broadcasted_iota(jnp.int32, (r, LANES), 1) // SSM_PAIR
    for c in range(ncol):
        steps = [h_scr[c, pl.ds(k, r, stride=SSM_CHUNK), :] for k in range(SSM_CHUNK)]
        for pp in range(QUAD):
            cols = []
            for kc in range(SSM_CHUNK // QUAD):
                acc = None
                for jq in range(QUAD):
                    v = steps[QUAD * kc + jq]
                    shift = ((jq - pp) * SSM_PAIR) % LANES
                    if shift:
                        v = pltpu.roll(v, shift, 1)
                    acc = v if acc is None else jnp.where(lane_q == jq, v, acc)
                cols.append(acc)
            u_ref[QUAD * c + pp] = jnp.concatenate(cols, axis=1).astype(BF16)


def _prenorm(x, g, mod, bc, n, b):
    nt = x.shape[0]
    tm = ROW_TILE
    r = tm // SSM_CHUNK
    row = pl.BlockSpec((tm, D_MODEL), lambda t: (t, 0))
    return pl.pallas_call(
        _prenorm_kernel,
        grid=(nt // tm,),
        in_specs=[row, pl.BlockSpec((1, D_MODEL), lambda t: (0, 0)),
                  _mod_spec(0, tm, bc, n, b), _mod_spec(1, tm, bc, n, b)],
        out_specs=pl.BlockSpec((N_PAIRS, r, PAIR_W), lambda t: (0, t, 0)),
        out_shape=jax.ShapeDtypeStruct((N_PAIRS, nt // SSM_CHUNK, PAIR_W), BF16),
        scratch_shapes=[pltpu.VMEM((D_MODEL // LANES, tm, LANES), F32)],
        compiler_params=_cparams(("parallel",)),
        name="s5_prenorm",
    )(x, g, mod, mod)


def _cmul_add(xr, xi, lr, li, tr, ti):
    return xr + lr * tr - li * ti, xi + lr * ti + li * tr


def _block_scan(er, ei, tab_r, tab_i, s_r, s_i, row, reverse):
    xr, xi = er, ei
    for k in (1, 2, 4):
        if reverse:
            keep = row < SCAN_ROWS - k
            lr, li = tab_r[SCAN_ROWS - k:SCAN_ROWS - k + 1], tab_i[SCAN_ROWS - k:SCAN_ROWS - k + 1]
            shift = SCAN_ROWS - k
        else:
            keep = row >= k
            lr, li = tab_r[k - 1:k], tab_i[k - 1:k]
            shift = k
        tr = jnp.where(keep, pltpu.roll(xr, shift, 0), 0.0)
        ti = jnp.where(keep, pltpu.roll(xi, shift, 0), 0.0)
        xr, xi = _cmul_add(xr, xi, lr, li, tr, ti)
    ar, ai = _cmul_add(xr, xi, tab_r, tab_i, s_r, s_i)
    if reverse:
        edge = row == SCAN_ROWS - 1
        inr = jnp.where(edge, s_r, pltpu.roll(ar, SCAN_ROWS - 1, 0))
        ini = jnp.where(edge, s_i, pltpu.roll(ai, SCAN_ROWS - 1, 0))
        return inr, ini, ar[0:1], ai[0:1]
    edge = row == 0
    inr = jnp.where(edge, s_r, pltpu.roll(ar, 1, 0))
    ini = jnp.where(edge, s_i, pltpu.roll(ai, 1, 0))
    return inr, ini, ar[SCAN_ROWS - 1:SCAN_ROWS], ai[SCAN_ROWS - 1:SCAN_ROWS]


def _s5_core_kernel(u_ref, w3_ref, w2t_ref, quad_ref, tab_ref, y_ref, e_scr, s_scr, tz_scr, *,
                    b, n_ctx_chunks, n_chunks):
    for k in range(SSM_CHUNK):
        for q in range(SSM_CHUNK // QUAD):
            tz_scr[k * SSM_PAIR:(k + 1) * SSM_PAIR, q * LANES:(q + 1) * LANES] = (
                quad_ref[QUAD * q - k + SSM_CHUNK - 1])
    u = u_ref[...]
    e_scr[...] = jnp.dot(u, w3_ref[...], preferred_element_type=F32)
    q4 = LANES
    tfr, tfi, trr, tri = tab_ref[0], tab_ref[1], tab_ref[2], tab_ref[3]
    row = lax.broadcasted_iota(jnp.int32, (SCAN_ROWS, q4), 0)
    nb_ctx = n_ctx_chunks // SCAN_ROWS
    nb = n_chunks // SCAN_ROWS

    def body(m, carry):
        mr = jnp.where(m < nb_ctx, nb_ctx - 1 - m, nb - 1 - (m - nb_ctx))
        n_lat_chunks = n_chunks - n_ctx_chunks

        def block_rows(bi, blk):
            start = jnp.where(blk < nb_ctx, bi * n_ctx_chunks + blk * SCAN_ROWS,
                              b * n_ctx_chunks + bi * n_lat_chunks + (blk - nb_ctx) * SCAN_ROWS)
            return pl.ds(pl.multiple_of(start, SCAN_ROWS), SCAN_ROWS)

        out = []
        for bi in range(b):
            sfr, sfi, srr, sri = carry[4 * bi:4 * bi + 4]
            rf = block_rows(bi, m)
            rr = block_rows(bi, mr)
            inr, ini, sfr, sfi = _block_scan(e_scr[rf, 0:q4], e_scr[rf, q4:2 * q4], tfr, tfi, sfr, sfi, row, False)
            s_scr[rf, 0:q4] = inr
            s_scr[rf, q4:2 * q4] = ini
            inr, ini, srr, sri = _block_scan(e_scr[rr, 2 * q4:3 * q4], e_scr[rr, 3 * q4:4 * q4], trr, tri,
                                             srr, sri, row, True)
            s_scr[rr, 2 * q4:3 * q4] = inr
            s_scr[rr, 3 * q4:4 * q4] = ini
            out += [sfr, sfi, srr, sri]
        return tuple(out)

    z = jnp.zeros((1, q4), F32)
    lax.fori_loop(0, nb, body, (z,) * (4 * b))
    y = jnp.dot(u, tz_scr[...], preferred_element_type=F32)
    y = y + lax.dot_general(s_scr[...].astype(BF16), w2t_ref[...], (((1,), (1,)), ((), ())),
                            preferred_element_type=F32)
    y_ref[...] = y.astype(BF16)


def _s5_core(u, w3, w2t, quads, tab, b, n_ctx_chunks, n_chunks):
    rows = u.shape[1]
    mat = pl.BlockSpec((None, PAIR_W, PAIR_W), lambda p: (p, 0, 0))
    seq = pl.BlockSpec((None, rows, PAIR_W), lambda p: (p, 0, 0))
    return pl.pallas_call(
        functools.partial(_s5_core_kernel, b=b, n_ctx_chunks=n_ctx_chunks, n_chunks=n_chunks),
        grid=(N_PAIRS,),
        in_specs=[seq, mat, mat,
                  pl.BlockSpec((None, N_QUADS, SSM_PAIR, LANES), lambda p: (p, 0, 0, 0)),
                  pl.BlockSpec((None, 4, SCAN_ROWS, LANES), lambda p: (p, 0, 0, 0))],
        out_specs=seq,
        out_shape=jax.ShapeDtypeStruct((N_PAIRS, rows, PAIR_W), BF16),
        scratch_shapes=[pltpu.VMEM((rows, PAIR_W), F32), pltpu.VMEM((rows, PAIR_W), F32),
                        pltpu.VMEM((PAIR_W, PAIR_W), BF16)],
        compiler_params=_cparams(("parallel",)),
        name="s5_core",
    )(u, w3, w2t, quads, tab)


def _s5_out_kernel(x_ref, yp_ref, g_ref, sh_ref, sc_ref, gt_ref, d_ref, w_ref, wb_ref, o_ref, y_scr):
    x = x_ref[...]
    tm = x.shape[0]
    r = tm // SSM_CHUNK
    lane_q = lax.broadcasted_iota(jnp.int32, (r, LANES), 1) // SSM_PAIR
    ncol = D_MODEL // LANES
    for c in range(ncol):
        for kc in range(SSM_CHUNK // QUAD):
            src = [yp_ref[QUAD * c + pp, :, kc * LANES:(kc + 1) * LANES].astype(F32) for pp in range(QUAD)]
            for jq in range(QUAD):
                acc = None
                for pp in range(QUAD):
                    v = src[pp]
                    shift = ((pp - jq) * SSM_PAIR) % LANES
                    if shift:
                        v = pltpu.roll(v, shift, 1)
                    acc = v if acc is None else jnp.where(lane_q == pp, v, acc)
                y_scr[c, pl.ds(QUAD * kc + jq, r, stride=SSM_CHUNK), :] = acc
    y = jnp.concatenate([y_scr[c] for c in range(ncol)], axis=1)
    h = _norm_mod(x, g_ref[...], sh_ref[...], sc_ref[...])
    yy = y + d_ref[...] * h
    z = jax.nn.gelu(yy).astype(BF16)
    ag = jnp.dot(z, w_ref[...], preferred_element_type=F32) + wb_ref[...]
    a = ag[:, :D_MODEL]
    g = ag[:, D_MODEL:]
    o_ref[...] = x + gt_ref[...] * (a * jax.nn.sigmoid(g))


def _s5_out(x, yp, g, mod, d, glu_w, glu_b, bc, n, b):
    nt = x.shape[0]
    tm = ROW_TILE
    row = pl.BlockSpec((tm, D_MODEL), lambda t: (t, 0))
    vec = pl.BlockSpec((1, D_MODEL), lambda t: (0, 0))
    return pl.pallas_call(
        _s5_out_kernel,
        grid=(nt // tm,),
        in_specs=[row, pl.BlockSpec((N_PAIRS, tm // SSM_CHUNK, PAIR_W), lambda t: (0, t, 0)), vec,
                  _mod_spec(0, tm, bc, n, b), _mod_spec(1, tm, bc, n, b), _mod_spec(2, tm, bc, n, b),
                  vec,
                  pl.BlockSpec((D_MODEL, 2 * D_MODEL), lambda t: (0, 0)),
                  pl.BlockSpec((1, 2 * D_MODEL), lambda t: (0, 0))],
        out_specs=row,
        out_shape=jax.ShapeDtypeStruct((nt, D_MODEL), F32),
        scratch_shapes=[pltpu.VMEM((D_MODEL // LANES, tm, LANES), F32)],
        compiler_params=_cparams(("parallel",)),
        name="s5_out",
    )(x, yp, g, mod, mod, mod, d, glu_w, glu_b)


def _pair_lanes(x):
    z = jnp.zeros_like(x[..., 0, :, :])
    top = jnp.concatenate([x[..., 0, :, :], z], axis=-1)
    bot = jnp.concatenate([z, x[..., 1, :, :]], axis=-1)
    return jnp.stack([top, bot], axis=-3)


def _s5_chunk_weights(a_re, a_im, log_dt, b_re, b_im, c_re, c_im):
    hp = lax.Precision.HIGHEST
    L = SSM_CHUNK
    G, P, I = SSM_GROUPS, SSM_STATE, SSM_GROUP
    a_re = a_re.astype(F32)
    a_im = a_im.astype(F32)
    dt = jnp.exp(log_dt.astype(F32))[..., None]
    mag = jnp.exp(a_re * dt)
    lam_re = mag * jnp.cos(a_im * dt)
    lam_im = mag * jnp.sin(a_im * dt)
    den = a_re * a_re + a_im * a_im
    num_re = lam_re - 1.0
    f_re = (num_re * a_re + lam_im * a_im) / den
    f_im = (lam_im * a_re - num_re * a_im) / den
    b_re = b_re.astype(F32)
    b_im = b_im.astype(F32)
    bb_re = f_re[..., None] * b_re - f_im[..., None] * b_im
    bb_im = f_re[..., None] * b_im + f_im[..., None] * b_re
    m = jnp.arange(L + 1, dtype=F32)[:, None, None, None]
    pmag = jnp.exp(m * (a_re * dt)[None])
    pw_re = pmag * jnp.cos(m * (a_im * dt)[None])
    pw_im = pmag * jnp.sin(m * (a_im * dt)[None])
    c_re = c_re.astype(F32)
    c_im = c_im.astype(F32)
    cl_re = c_re[None] * pw_re[:, :, :, None, :] - c_im[None] * pw_im[:, :, :, None, :]
    cl_im = c_re[None] * pw_im[:, :, :, None, :] + c_im[None] * pw_re[:, :, :, None, :]
    bt_re = jnp.swapaxes(bb_re, -1, -2)
    bt_im = jnp.swapaxes(bb_im, -1, -2)
    lb_re = pw_re[:, :, :, None, :] * bt_re[None] - pw_im[:, :, :, None, :] * bt_im[None]
    lb_im = pw_re[:, :, :, None, :] * bt_im[None] + pw_im[:, :, :, None, :] * bt_re[None]

    def pair_rows(parts):
        cols = []
        for x in parts:
            x = x.reshape(L, N_PAIRS, 2, x.shape[-2], P)
            cols.append(_pair_lanes(jnp.transpose(x, (1, 0, 2, 3, 4))))
        return jnp.concatenate(cols, axis=-1).reshape(N_PAIRS, PAIR_W, 4 * 2 * P)

    pf = (L - 1) - jnp.arange(L)
    pr = jnp.arange(L)
    w3 = pair_rows([lb_re[pf, 0], lb_im[pf, 0], lb_re[pr, 1], lb_im[pr, 1]])
    qf = jnp.arange(L) + 1
    qr = L - jnp.arange(L)
    w2t = pair_rows([cl_re[qf, 0], -cl_im[qf, 0], cl_re[qr, 1], -cl_im[qr, 1]])
    kk = (jnp.einsum('mrgop,rgpi->mrgio', cl_re[:L], bb_re, precision=hp)
          - jnp.einsum('mrgop,rgpi->mrgio', cl_im[:L], bb_im, precision=hp))
    kp = _pair_lanes(kk.reshape(L, 2, N_PAIRS, 2, I, I)).reshape(L, 2, N_PAIRS, SSM_PAIR, SSM_PAIR)
    lags = jnp.concatenate([kp[:0:-1, 1], kp[:1, 0] + kp[:1, 1], kp[1:, 0]], axis=0)
    quads = jnp.concatenate([lags[q:q + N_QUADS] for q in range(QUAD)], axis=-1)
    quads = jnp.transpose(quads, (1, 0, 2, 3))
    jf = (jnp.arange(SCAN_ROWS, dtype=F32) + 1.0) * L
    jr = (SCAN_ROWS - jnp.arange(SCAN_ROWS, dtype=F32)) * L
    jj = jnp.stack([jf, jr], axis=0)[:, :, None, None]
    tmag = jnp.exp(jj * (a_re * dt)[:, None])
    t_re = tmag * jnp.cos(jj * (a_im * dt)[:, None])
    t_im = tmag * jnp.sin(jj * (a_im * dt)[:, None])
    tab = jnp.stack([t_re[0], t_im[0], t_re[1], t_im[1]], axis=0)
    tab = jnp.transpose(tab.reshape(4, SCAN_ROWS, N_PAIRS, 2 * P), (2, 0, 1, 3))
    return w3.astype(BF16), w2t.astype(BF16), quads.astype(BF16), tab


def _s5_layer(x, g, mod, ssm, bc, n, b):
    (a_re, a_im, log_dt, b_re, b_im, c_re, c_im, d, glu_w, glu_b) = ssm
    c_len = bc // b
    n_ctx_chunks = c_len // SSM_CHUNK
    n_chunks = (c_len + n) // SSM_CHUNK
    u = _prenorm(x, g, mod, bc, n, b)
    w3, w2t, quads, tab = _s5_chunk_weights(a_re, a_im, log_dt, b_re, b_im, c_re, c_im)
    yp = _s5_core(u, w3, w2t, quads, tab, b, n_ctx_chunks, n_chunks)
    return _s5_out(x, yp, g, mod, d.reshape(1, D_MODEL), glu_w.astype(BF16),
                   glu_b.reshape(1, 2 * D_MODEL), bc, n, b)


def _ffn_kernel(x_ref, g_ref, sh_ref, sc_ref, gt_ref, wg_ref, wu_ref, wd_ref, o_ref, h_scr, acc_scr):
    j = pl.program_id(1)

    @pl.when(j == 0)
    def _():
        h_scr[...] = _norm_mod(x_ref[...], g_ref[...], sh_ref[...], sc_ref[...]).astype(BF16)
        acc_scr[...] = jnp.zeros_like(acc_scr)

    h = h_scr[...]
    gate = jnp.dot(h, wg_ref[...], preferred_element_type=F32)
    up = jnp.dot(h, wu_ref[...], preferred_element_type=F32)
    a = (gate * jax.nn.sigmoid(gate) * up).astype(BF16)
    acc_scr[...] += jnp.dot(a, wd_ref[...], preferred_element_type=F32)

    @pl.when(j == pl.num_programs(1) - 1)
    def _():
        o_ref[...] = x_ref[...] + gt_ref[...] * acc_scr[...]


def _ffn(x, g, mod, w_gate_up, w_down, bc, n, b):
    nt = x.shape[0]
    tm = ROW_TILE
    nj = D_FF // FF_TILE
    row = pl.BlockSpec((tm, D_MODEL), lambda t, j: (t, 0))
    return pl.pallas_call(
        _ffn_kernel,
        grid=(nt // tm, nj),
        in_specs=[row, pl.BlockSpec((1, D_MODEL), lambda t, j: (0, 0)),
                  _mod_spec(3, tm, bc, n, b), _mod_spec(4, tm, bc, n, b), _mod_spec(5, tm, bc, n, b),
                  pl.BlockSpec((D_MODEL, FF_TILE), lambda t, j: (0, j)),
                  pl.BlockSpec((D_MODEL, FF_TILE), lambda t, j: (0, nj + j)),
                  pl.BlockSpec((FF_TILE, D_MODEL), lambda t, j: (j, 0))],
        out_specs=row,
        out_shape=jax.ShapeDtypeStruct((nt, D_MODEL), F32),
        scratch_shapes=[pltpu.VMEM((tm, D_MODEL), BF16), pltpu.VMEM((tm, D_MODEL), F32)],
        compiler_params=_cparams(("parallel", "arbitrary")),
        name="ffn_dense",
    )(x, g, mod, mod, mod, w_gate_up, w_gate_up, w_down)


def _head_ms(sq, ones):
    return jnp.dot(sq.astype(BF16), ones, preferred_element_type=F32)


def _rope(v, cos, sin_signed, first_half):
    w = v.shape[-1]
    quarter = HEAD_DIM // 4
    swapped = jnp.where(first_half, pltpu.roll(v, w - quarter, 1), pltpu.roll(v, quarter, 1))
    return v * cos + swapped * sin_signed


def _qkv_kernel(x_ref, g_ref, sh_ref, sc_ref, w_ref, qg_ref, kg_ref, ones_ref, cos_ref, sin_ref,
                q_ref, k_ref, v_ref):
    tm = x_ref.shape[0]
    h = _norm_mod(x_ref[...], g_ref[...], sh_ref[...], sc_ref[...]).astype(BF16)
    qkv = jnp.dot(h, w_ref[...], preferred_element_type=F32)
    dq = N_Q_HEADS * HEAD_DIM
    dk = N_KV_HEADS * HEAD_DIM
    q = qkv[:, :dq]
    k = qkv[:, dq:dq + dk]
    v = qkv[:, dq + dk:]
    ones = ones_ref[...]
    q = q * lax.rsqrt(_head_ms(q * q, ones) + EPS) * qg_ref[...]
    k = k * lax.rsqrt(_head_ms(k * k, ones[:dk, :dk]) + EPS) * kg_ref[...]
    cos = cos_ref[...]
    sin = sin_ref[...]
    lane_q = lax.broadcasted_iota(jnp.int32, (tm, dq), 1)
    lane_k = lax.broadcasted_iota(jnp.int32, (tm, dk), 1)
    half = HEAD_DIM // 2
    quarter = HEAD_DIM // 4
    q = _rope(q, jnp.tile(cos, (1, dq // LANES)), jnp.tile(sin, (1, dq // LANES)), (lane_q % half) < quarter)
    k = _rope(k, jnp.tile(cos, (1, dk // LANES)), jnp.tile(sin, (1, dk // LANES)), (lane_k % half) < quarter)
    q = q * (HEAD_DIM ** -0.5)
    k_ref[...] = k.astype(BF16)
    lane = lax.broadcasted_iota(jnp.int32, (tm, LANES), 1)
    low = lane < HEAD_DIM
    for j in range(N_KV_HEADS):
        vc = v[:, (j // 2) * LANES:(j // 2 + 1) * LANES]
        if j % 2:
            vc = pltpu.roll(vc, HEAD_DIM, 1)
        v_ref[:, j * LANES:(j + 1) * LANES] = jnp.where(low, vc, 1.0).astype(BF16)
    for c in range(dq // LANES):
        vc = q[:, c * LANES:(c + 1) * LANES]
        vr = pltpu.roll(vc, HEAD_DIM, 1)
        kv_half = (c // 2) % 2
        if kv_half == 0:
            first = jnp.where(low, vc, 0.0)
            second = jnp.where(low, vr, 0.0)
        else:
            first = jnp.where(low, 0.0, vr)
            second = jnp.where(low, 0.0, vc)
        q_ref[:, (2 * c) * LANES:(2 * c + 1) * LANES] = first.astype(BF16)
        q_ref[:, (2 * c + 1) * LANES:(2 * c + 2) * LANES] = second.astype(BF16)


def _qkv(x, g, mod, w_qkv, q_g, k_g, ones, cos, sin, bc, n, b):
    nt = x.shape[0]
    tm = ROW_TILE
    dq = N_Q_HEADS * HEAD_DIM
    dk = N_KV_HEADS * HEAD_DIM
    row = lambda w: pl.BlockSpec((tm, w), lambda t: (t, 0))
    full = lambda r, w: pl.BlockSpec((r, w), lambda t: (0, 0))
    return pl.pallas_call(
        _qkv_kernel,
        grid=(nt // tm,),
        in_specs=[row(D_MODEL), full(1, D_MODEL), _mod_spec(0, tm, bc, n, b), _mod_spec(1, tm, bc, n, b),
                  full(D_MODEL, dq + 2 * dk), full(1, dq), full(1, dk), full(dq, dq),
                  row(LANES), row(LANES)],
        out_specs=[row(2 * dq), row(dk), row(N_KV_HEADS * LANES)],
        out_shape=[jax.ShapeDtypeStruct((nt, 2 * dq), BF16),
                   jax.ShapeDtypeStruct((nt, dk), BF16),
                   jax.ShapeDtypeStruct((nt, N_KV_HEADS * LANES), BF16)],
        compiler_params=_cparams(("parallel",)),
        name="attn_qkv",
    )(x, g, mod, mod, w_qkv, q_g, k_g, ones, cos, sin)


def _attn_kernel(*refs, sub, n_sub, n_lat):
    if n_lat:
        q_ref, kc_ref, vc_ref, kl_ref, vl_ref, o_ref = refs[:6]
    else:
        q_ref, kc_ref, vc_ref, o_ref = refs[:4]
    s_bufs = refs[-2 * ATTN_DEPTH:-ATTN_DEPTH]
    m_bufs = refs[-ATTN_DEPTH:]
    nt_dims = (((1,), (1,)), ((), ()))
    c_len = kc_ref.shape[0]
    parts = [(kc_ref, vc_ref, 0, 0, c_len)]
    if n_lat:
        kc = min(ATTN_KEY_CHUNK, n_lat)
        parts += [(kl_ref, vl_ref, c0, c_len + c0, kc) for c0 in range(0, n_lat, kc)]
    lane = lax.broadcasted_iota(jnp.int32, (sub, LANES), 1)
    low = lane < HEAD_DIM

    def scores(i, s_scr, m_scr):
        r0 = pl.multiple_of(i * sub, sub)
        qs = jnp.concatenate([q_ref[pl.ds(r0, sub), a * LANES:(a + 1) * LANES] for a in range(Q_PER_KV)],
                             axis=0)
        m = None
        for kr, _, k0, off, sz in parts:
            s = lax.dot_general(qs, kr[k0:k0 + sz, :], nt_dims, preferred_element_type=F32)
            s_scr[:, off:off + sz] = s
            mx = s.max(axis=-1, keepdims=True)
            m = mx if m is None else jnp.maximum(m, mx)
        m_scr[...] = m

    def output(i, s_scr, m_scr):
        r0 = pl.multiple_of(i * sub, sub)
        m = m_scr[...]
        acc = jnp.zeros((Q_PER_KV * sub, LANES), F32)
        for _, vr, k0, off, sz in parts:
            p = jnp.exp(s_scr[:, off:off + sz] - m)
            acc = acc + jnp.dot(p.astype(BF16), vr[k0:k0 + sz, :], preferred_element_type=F32)
        o = acc / pltpu.roll(acc, HEAD_DIM, 1)
        for c in range(Q_PER_KV // 2):
            a0 = o[(2 * c) * sub:(2 * c + 1) * sub]
            a1 = o[(2 * c + 1) * sub:(2 * c + 2) * sub]
            o_ref[pl.ds(r0, sub), c * LANES:(c + 1) * LANES] = (
                jnp.where(low, a0, pltpu.roll(a1, HEAD_DIM, 1)).astype(BF16))

    depth = ATTN_DEPTH
    ahead = depth - 1

    def stage(i, r):
        output(i, s_bufs[r], m_bufs[r])
        nxt = (r + ahead) % depth
        scores(i + ahead, s_bufs[nxt], m_bufs[nxt])

    for t in range(min(ahead, n_sub)):
        scores(t, s_bufs[t], m_bufs[t])
    n_loop = max(n_sub - ahead, 0) // depth

    def body(j, carry):
        for r in range(depth):
            stage(j * depth + r, r)
        return carry

    if n_loop:
        lax.fori_loop(0, n_loop, body, 0)
    for i in range(n_loop * depth, n_sub):
        if i + ahead < n_sub:
            stage(i, i % depth)
        else:
            output(i, s_bufs[i % depth], m_bufs[i % depth])


def _attn_call(name, tq, n_lat, c_len, grid, in_specs, out_spec, out_rows, args):
    rows = Q_PER_KV * ATTN_SUB
    assert tq % ATTN_SUB == 0
    return pl.pallas_call(
        functools.partial(_attn_kernel, sub=ATTN_SUB, n_sub=tq // ATTN_SUB, n_lat=n_lat),
        grid=grid,
        in_specs=in_specs,
        out_specs=out_spec,
        out_shape=jax.ShapeDtypeStruct((out_rows, D_MODEL), BF16),
        scratch_shapes=([pltpu.VMEM((rows, c_len + n_lat), F32)] * ATTN_DEPTH
                        + [pltpu.VMEM((rows, 1), F32)] * ATTN_DEPTH),
        compiler_params=_cparams(("parallel", "parallel", "parallel")),
        name=name,
    )(*args)


def _attn_latent(qz, k, v, bc, n, b):
    c_len = bc // b
    tq = min(Q_TILE, n)
    assert bc % tq == 0 and n % tq == 0
    gw = Q_PER_KV * LANES
    k_lat, v_lat = k[bc:], v[bc:]
    q_spec = pl.BlockSpec((tq, gw), lambda bi, g, qi: ((bc + bi * n) // tq + qi, g))
    ctx_spec = pl.BlockSpec((c_len, LANES), lambda bi, g, qi: (bi, g // 2))
    lat_spec = pl.BlockSpec((n, LANES), lambda bi, g, qi: (bi, g // 2))
    vctx_spec = pl.BlockSpec((c_len, LANES), lambda bi, g, qi: (bi, g))
    vlat_spec = pl.BlockSpec((n, LANES), lambda bi, g, qi: (bi, g))
    o_spec = pl.BlockSpec((tq, Q_PER_KV * HEAD_DIM), lambda bi, g, qi: (bi * (n // tq) + qi, g))
    return _attn_call("attn_latent", tq, n, c_len, (b, N_KV_HEADS, n // tq),
                      [q_spec, ctx_spec, vctx_spec, lat_spec, vlat_spec], o_spec, b * n,
                      (qz, k, v, k_lat, v_lat))


def _attn_ctx(qz, k, v, bc, b):
    c_len = bc // b
    gw = Q_PER_KV * LANES
    q_spec = pl.BlockSpec((c_len, gw), lambda bi, g, qi: (bi, g))
    ctx_spec = pl.BlockSpec((c_len, LANES), lambda bi, g, qi: (bi, g // 2))
    vctx_spec = pl.BlockSpec((c_len, LANES), lambda bi, g, qi: (bi, g))
    o_spec = pl.BlockSpec((c_len, Q_PER_KV * HEAD_DIM), lambda bi, g, qi: (bi, g))
    return _attn_call("attn_ctx", c_len, 0, c_len, (b, N_KV_HEADS, 1),
                      [q_spec, ctx_spec, vctx_spec], o_spec, bc, (qz, k, v))


def _attn_out_kernel(o_ref, x_ref, wo_ref, gt_ref, g_ref, sh_ref, sc_ref, rw_ref, rb_ref,
                     xo_ref, h_ref, r_ref):
    tm = x_ref.shape[0]
    x = x_ref[...] + gt_ref[...] * jnp.dot(o_ref[...], wo_ref[...], preferred_element_type=F32)
    xo_ref[...] = x
    h = _norm_mod(x, g_ref[...], sh_ref[...], sc_ref[...])
    h_ref[...] = h
    logits = jnp.dot(h, rw_ref[...], preferred_element_type=F32, precision=lax.Precision.HIGHEST) + rb_ref[...]
    lane = lax.broadcasted_iota(jnp.int32, (tm, LANES), 1)
    neg = -jnp.inf
    logits = jnp.where(lane < N_EXPERTS, logits, neg)
    m1 = jnp.max(logits, axis=-1, keepdims=True)
    i1 = jnp.min(jnp.where(logits == m1, lane, LANES), axis=-1, keepdims=True)
    rest = jnp.where(lane == i1, neg, logits)
    m2 = jnp.max(rest, axis=-1, keepdims=True)
    i2 = jnp.min(jnp.where(rest == m2, lane, LANES), axis=-1, keepdims=True)
    e = jnp.exp(m2 - m1)
    w1 = 1.0 / (1.0 + e)
    w2 = e / (1.0 + e)
    r = jnp.where(lane == 0, i1.astype(F32), 0.0)
    r = jnp.where(lane == 1, i2.astype(F32), r)
    r = jnp.where(lane == 2, w1, r)
    r = jnp.where(lane == 3, w2, r)
    r_ref[...] = r


def _attn_out(o, x, w_o, g, mod, router_w, router_b, bc, n, b):
    nt = x.shape[0]
    tm = ROW_TILE
    row = lambda w: pl.BlockSpec((tm, w), lambda t: (t, 0))
    full = lambda r, w: pl.BlockSpec((r, w), lambda t: (0, 0))
    return pl.pallas_call(
        _attn_out_kernel,
        grid=(nt // tm,),
        in_specs=[row(D_MODEL), row(D_MODEL), full(D_MODEL, D_MODEL), _mod_spec(2, tm, bc, n, b),
                  full(1, D_MODEL), _mod_spec(3, tm, bc, n, b), _mod_spec(4, tm, bc, n, b),
                  full(D_MODEL, LANES), full(1, LANES)],
        out_specs=[row(D_MODEL), row(D_MODEL), row(LANES)],
        out_shape=[jax.ShapeDtypeStruct((nt, D_MODEL), F32),
                   jax.ShapeDtypeStruct((nt, D_MODEL), F32),
                   jax.ShapeDtypeStruct((nt, LANES), F32)],
        compiler_params=_cparams(("parallel",)),
        name="attn_out_router",
    )(o, x, w_o, mod, g, mod, mod, router_w, router_b)


def _moe_ffn_kernel(te_ref, tv_ref, x_ref, wg_ref, wu_ref, wd_ref, o_ref, h_scr, acc_scr):
    i = pl.program_id(0)
    j = pl.program_id(1)
    valid = tv_ref[i] > 0

    @pl.when(j == 0)
    def _():
        h_scr[...] = x_ref[...].astype(BF16)
        acc_scr[...] = jnp.zeros_like(acc_scr)

    @pl.when(valid)
    def _():
        h = h_scr[...]
        gate = jnp.dot(h, wg_ref[...], preferred_element_type=F32)
        up = jnp.dot(h, wu_ref[...], preferred_element_type=F32)
        a = (gate * jax.nn.sigmoid(gate) * up).astype(BF16)
        acc_scr[...] += jnp.dot(a, wd_ref[...], preferred_element_type=F32)

    @pl.when(j == pl.num_programs(1) - 1)
    def _():
        o_ref[...] = acc_scr[...].astype(BF16)


def _moe_ffn(xs, tile_expert, tile_valid, w_gate_up, w_down):
    slots = xs.shape[0]
    tm = MOE_TILE
    nj = D_FF // FF_TILE
    row = pl.BlockSpec((tm, D_MODEL), lambda i, j, te, tv: (i, 0))
    grid_spec = pltpu.PrefetchScalarGridSpec(
        num_scalar_prefetch=2,
        grid=(slots // tm, nj),
        in_specs=[row,
                  pl.BlockSpec((None, D_MODEL, FF_TILE), lambda i, j, te, tv: (te[i], 0, j)),
                  pl.BlockSpec((None, D_MODEL, FF_TILE), lambda i, j, te, tv: (te[i], 0, nj + j)),
                  pl.BlockSpec((None, FF_TILE, D_MODEL), lambda i, j, te, tv: (te[i], j, 0))],
        out_specs=row,
        scratch_shapes=[pltpu.VMEM((tm, D_MODEL), BF16), pltpu.VMEM((tm, D_MODEL), F32)])
    return pl.pallas_call(
        _moe_ffn_kernel,
        grid_spec=grid_spec,
        out_shape=jax.ShapeDtypeStruct((slots, D_MODEL), BF16),
        compiler_params=_cparams(("parallel", "arbitrary")),
        name="moe_ffn",
    )(tile_expert, tile_valid, xs, w_gate_up, w_gate_up, w_down)


def _moe_combine_kernel(x_ref, a_ref, b_ref, r_ref, gt_ref, o_ref):
    tm = x_ref.shape[0]
    lane = lax.broadcasted_iota(jnp.int32, (tm, LANES), 1)
    r = r_ref[...]
    w1 = jnp.sum(jnp.where(lane == 2, r, 0.0), axis=-1, keepdims=True)
    w2 = jnp.sum(jnp.where(lane == 3, r, 0.0), axis=-1, keepdims=True)
    y = w1 * a_ref[...].astype(F32) + w2 * b_ref[...].astype(F32)
    o_ref[...] = x_ref[...] + gt_ref[...] * y


def _moe_combine(x, ya, yb, route, mod, bc, n, b):
    nt = x.shape[0]
    tm = ROW_TILE
    row = lambda w: pl.BlockSpec((tm, w), lambda t: (t, 0))
    return pl.pallas_call(
        _moe_combine_kernel,
        grid=(nt // tm,),
        in_specs=[row(D_MODEL), row(D_MODEL), row(D_MODEL), row(LANES), _mod_spec(5, tm, bc, n, b)],
        out_specs=row(D_MODEL),
        out_shape=jax.ShapeDtypeStruct((nt, D_MODEL), F32),
        compiler_params=_cparams(("parallel",)),
        name="moe_combine",
    )(x, ya, yb, route, mod)


def _moe_layer(x, h, route, mod, w_gate_up, w_down, bc, n, b):
    nt = x.shape[0]
    tm = MOE_TILE
    top_i = route[:, :TOP_K].astype(jnp.int32)
    flat_e = top_i.reshape(-1)
    onehot = (flat_e[:, None] == jnp.arange(N_EXPERTS)[None, :]).astype(jnp.int32)
    counts = jnp.sum(onehot, axis=0)
    rank = jnp.sum((jnp.cumsum(onehot, axis=0) - onehot) * onehot, axis=1)
    padded = ((counts + tm - 1) // tm) * tm
    ends = jnp.cumsum(padded)
    starts = ends - padded
    pos = starts[flat_e] + rank
    n_tiles = (TOP_K * nt) // tm + N_EXPERTS
    slots = n_tiles * tm
    token_of_slot = jnp.zeros((slots,), jnp.int32).at[pos].set(jnp.arange(TOP_K * nt, dtype=jnp.int32) // TOP_K)
    tile_start = jnp.arange(n_tiles, dtype=jnp.int32) * tm
    tile_expert = jnp.sum((tile_start[:, None] >= ends[None, :]).astype(jnp.int32), axis=1)
    tile_valid = (tile_start < ends[-1]).astype(jnp.int32)
    last_expert = jnp.max(jnp.where(counts > 0, jnp.arange(N_EXPERTS), 0))
    tile_expert = jnp.where(tile_valid > 0, tile_expert, last_expert).astype(jnp.int32)
    xs = h.at[token_of_slot].get(mode="promise_in_bounds")
    ys = _moe_ffn(xs, tile_expert, tile_valid, w_gate_up, w_down)
    pos2 = pos.reshape(nt, TOP_K)
    ya = ys.at[pos2[:, 0]].get(mode="promise_in_bounds")
    yb = ys.at[pos2[:, 1]].get(mode="promise_in_bounds")
    return _moe_combine(x, ya, yb, route, mod, bc, n, b)


def _rope_tables(bc, n):
    rows = n // GRID_W
    row = jnp.broadcast_to(jnp.arange(rows)[:, None], (rows, GRID_W)).reshape(-1).astype(F32)
    col = jnp.broadcast_to(jnp.arange(GRID_W)[None, :], (rows, GRID_W)).reshape(-1).astype(F32)
    axis_dim = HEAD_DIM // 2
    inv = ROPE_THETA ** (-jnp.arange(0, axis_dim, 2, dtype=F32) / axis_dim)
    ar = row[:, None] * inv
    ac = col[:, None] * inv
    cos = jnp.concatenate([jnp.cos(ar), jnp.cos(ar), jnp.cos(ac), jnp.cos(ac)], axis=1)
    sin = jnp.concatenate([-jnp.sin(ar), jnp.sin(ar), -jnp.sin(ac), jnp.sin(ac)], axis=1)
    cos = jnp.tile(cos, (1, LANES // HEAD_DIM))
    sin = jnp.tile(sin, (1, LANES // HEAD_DIM))
    return cos, sin


def kernel(x, c, ctx, c_ctx, ada_w, ada_b, norm_mix_g, norm_ffn_g, ssm_a_re, ssm_a_im, ssm_log_dt,
           ssm_b_re, ssm_b_im, ssm_c_re, ssm_c_im, ssm_d, ssm_glu_w, ssm_glu_b, attn_w_qkv, attn_q_g,
           attn_k_g, attn_w_o, ffn_w_gate_up, ffn_w_down, moe_router_w, moe_router_b, moe_w_gate_up,
           moe_w_down):
    b, n, _ = x.shape
    c_len = ctx.shape[1]
    bc = b * c_len
    assert b + 1 <= MOD_ROWS and bc % ROW_TILE == 0 and n % ROW_TILE == 0
    assert c_len % SSM_CHUNK == 0 and n % SSM_CHUNK == 0 and n % GRID_W == 0

    cvecs = jnp.zeros((MOD_ROWS, D_MODEL), F32).at[:b].set(c).at[b].set(c_ctx)
    mods = _modulation_table(cvecs, ada_w, ada_b).reshape(DEPTH, MOD_ROWS, 1, N_MOD * D_MODEL)

    xs = jnp.concatenate([ctx.reshape(bc, D_MODEL), x.reshape(b * n, D_MODEL)], axis=0)

    cos_l, sin_l = _rope_tables(bc, n)
    cos = jnp.concatenate([jnp.ones((bc, LANES), F32), jnp.tile(cos_l, (b, 1))], axis=0)
    sin = jnp.concatenate([jnp.zeros((bc, LANES), F32), jnp.tile(sin_l, (b, 1))], axis=0)
    dq = N_Q_HEADS * HEAD_DIM
    hid = jnp.arange(dq) // HEAD_DIM
    ones = ((hid[:, None] == hid[None, :]).astype(F32) / HEAD_DIM).astype(BF16)

    for i in range(DEPTH):
        j = i // 2
        last = i == DEPTH - 1
        mod = mods[i]
        g_mix = norm_mix_g[i].reshape(1, D_MODEL)
        g_ffn = norm_ffn_g[i].reshape(1, D_MODEL)
        if i % 2 == 0:
            ssm = (ssm_a_re[j], ssm_a_im[j], ssm_log_dt[j], ssm_b_re[j], ssm_b_im[j], ssm_c_re[j],
                   ssm_c_im[j], ssm_d[j], ssm_glu_w[j], ssm_glu_b[j])
            xs = _s5_layer(xs, g_mix, mod, ssm, bc, n, b)
            xs = _ffn(xs, g_ffn, mod, _to_bf16(ffn_w_gate_up, j), _to_bf16(ffn_w_down, j), bc, n, b)
        else:
            qz, k, v = _qkv(xs, g_mix, mod, attn_w_qkv[j].astype(BF16),
                            jnp.tile(attn_q_g[j], N_Q_HEADS).reshape(1, dq),
                            jnp.tile(attn_k_g[j], N_KV_HEADS).reshape(1, N_KV_HEADS * HEAD_DIM),
                            ones, cos, sin, bc, n, b)
            o = _attn_latent(qz, k, v, bc, n, b)
            if last:
                xs = xs[bc:]
                lay = (0, n, b)
            else:
                o = jnp.concatenate([_attn_ctx(qz, k, v, bc, b), o], axis=0)
                lay = (bc, n, b)
            rw = jnp.zeros((D_MODEL, LANES), F32).at[:, :N_EXPERTS].set(moe_router_w[j])
            rb = jnp.zeros((1, LANES), F32).at[0, :N_EXPERTS].set(moe_router_b[j])
            xs, h, route = _attn_out(o, xs, attn_w_o[j].astype(BF16), g_ffn, mod, rw, rb, *lay)
            xs = _moe_layer(xs, h, route, mod, _to_bf16(moe_w_gate_up, j), _to_bf16(moe_w_down, j), *lay)
    return xs.reshape(b, n, D_MODEL)
```
